```python
import math
import jax, jax.numpy as jnp
from jax import lax
import numpy as np

D_MODEL = 2048
BATCH = 4
SEQ = 2048
DEPTH = 2

D_MIX = D_MODEL
D_RNN = D_MIX // 2
RNN_BLOCKS = 16
RNN_BLOCK = D_RNN // RNN_BLOCKS
CONV_W = 4
LRU_C = 8.0
ATT_HEADS = 8
ATT_DH = 64
ATT_DV = 2 * ATT_DH
D_ATT = ATT_HEADS * ATT_DV
D_Q = ATT_HEADS * 2 * ATT_DH
D_IN = 2 * D_RNN + 2 * D_Q + D_ATT
D_FF = 5632
ROPE_THETA = 10000.0
QBLK = 128
EPS = 1e-6
NEG_INF = -1e30

kernel_name = "hymba_rglru_diffattn_macaron"


def rms_norm(x, g):
    xf = x.astype(jnp.float32)
    y = xf * lax.rsqrt(jnp.mean(xf * xf, axis=-1, keepdims=True) + EPS)
    return (y * g.astype(jnp.float32)).astype(x.dtype)


def swiglu(h, w_gate, w_up, w_down):
    return (jax.nn.silu(h @ w_gate) * (h @ w_up)) @ w_down


def rope_tables(seq, dim):
    inv_freq = ROPE_THETA ** (-jnp.arange(0, dim, 2, dtype=jnp.float32) / dim)
    pos = jnp.arange(seq, dtype=jnp.float32)
    ang = pos[:, None] * inv_freq[None, :]
    ang = jnp.concatenate([ang, ang], axis=-1)
    return jnp.cos(ang), jnp.sin(ang)


def apply_rope(t, cos, sin):
    c = cos[:, None, None, :].astype(t.dtype)
    s = sin[:, None, None, :].astype(t.dtype)
    half = t.shape[-1] // 2
    rot = jnp.concatenate([-t[..., half:], t[..., :half]], axis=-1)
    return t * c + rot * s


def causal_depthwise_conv(x, w, b):
    s = x.shape[1]
    xp = jnp.pad(x, ((0, 0), (CONV_W - 1, 0), (0, 0)))
    y = b
    for k in range(CONV_W):
        y = y + xp[:, k:k + s, :] * w[k]
    return y


def rg_lru(x, w_a, b_a, w_x, b_x, lru_param):
    bsz, s, _ = x.shape
    xb = x.reshape(bsz, s, RNN_BLOCKS, RNN_BLOCK)
    r = jax.nn.sigmoid(jnp.einsum('bsgi,gij->bsgj', xb, w_a) + b_a).reshape(bsz, s, D_RNN)
    i = jax.nn.sigmoid(jnp.einsum('bsgi,gij->bsgj', xb, w_x) + b_x).reshape(bsz, s, D_RNN)
    log_a = -LRU_C * r.astype(jnp.float32) * jax.nn.softplus(-lru_param.astype(jnp.float32))
    a = jnp.exp(log_a)
    mult = jnp.sqrt(-jnp.expm1(2.0 * log_a))
    u = mult * (i * x).astype(jnp.float32)

    def combine(left, right):
        a1, b1 = left
        a2, b2 = right
        return a1 * a2, a2 * b1 + b2

    _, h = lax.associative_scan(combine, (a, u), axis=1)
    return h.astype(x.dtype)


def diff_attention(q, k, v, lam):
    bsz, s = q.shape[0], q.shape[1]
    nqb = s // QBLK
    scale = 1.0 / math.sqrt(ATT_DH)
    qb = q.reshape(bsz, nqb, QBLK, ATT_HEADS, 2, ATT_DH).transpose(1, 0, 2, 3, 4, 5)
    kpos = jnp.arange(s)

    def block(args):
        q_blk, j = args
        qpos = j * QBLK + jnp.arange(QBLK)
        sc = jnp.einsum('bqhcd,bkhcd->bhcqk', q_blk, k).astype(jnp.float32) * scale
        mask = kpos[None, :] <= qpos[:, None]
        sc = jnp.where(mask, sc, NEG_INF)
        p = jax.nn.softmax(sc, axis=-1)
        w = p[:, :, 0] - lam * p[:, :, 1]
        return jnp.einsum('bhqk,bkhe->bqhe', w.astype(v.dtype), v)

    out = lax.map(block, (qb, jnp.arange(nqb)))
    return out.transpose(1, 0, 2, 3, 4).reshape(bsz, s, ATT_HEADS, ATT_DV)


def setup_inputs(seed: int = 0) -> dict:
    key = jax.random.key(seed)
    ks = iter(jax.random.split(key, 64))

    def nrm(shape, scale):
        return jax.random.normal(next(ks), shape, jnp.float32) * scale

    def gain(shape):
        return 1.0 + nrm(shape, 0.01)

    L = DEPTH
    a_c = jax.random.uniform(next(ks), (L, D_RNN), jnp.float32, 0.9, 0.999)
    log_a = jnp.log(a_c) / LRU_C
    lru_param = log_a - jnp.log(-jnp.expm1(log_a))

    return {
        "x": nrm((BATCH, SEQ, D_MODEL), 1.0),
        "ffn1_norm": gain((L, D_MODEL)),
        "ffn1_w_gate": nrm((L, D_MODEL, D_FF), D_MODEL ** -0.5),
        "ffn1_w_up": nrm((L, D_MODEL, D_FF), D_MODEL ** -0.5),
        "ffn1_w_down": nrm((L, D_FF, D_MODEL), D_FF ** -0.5),
        "mix_norm": gain((L, D_MODEL)),
        "w_in": nrm((L, D_MODEL, D_IN), D_MODEL ** -0.5),
        "conv_w": nrm((L, CONV_W, D_RNN), CONV_W ** -0.5),
        "conv_b": nrm((L, D_RNN), 0.01),
        "gate_a_w": nrm((L, RNN_BLOCKS, RNN_BLOCK, RNN_BLOCK), RNN_BLOCK ** -0.5),
        "gate_a_b": nrm((L, RNN_BLOCKS, RNN_BLOCK), 0.01),
        "gate_x_w": nrm((L, RNN_BLOCKS, RNN_BLOCK, RNN_BLOCK), RNN_BLOCK ** -0.5),
        "gate_x_b": nrm((L, RNN_BLOCKS, RNN_BLOCK), 0.01),
        "lru_param": lru_param,
        "rnn_out_norm": gain((L, D_RNN)),
        "q_norm": gain((L, ATT_DH)),
        "k_norm": gain((L, ATT_DH)),
        "lam_q1": nrm((L, ATT_DH), 0.1),
        "lam_k1": nrm((L, ATT_DH), 0.1),
        "lam_q2": nrm((L, ATT_DH), 0.1),
        "lam_k2": nrm((L, ATT_DH), 0.1),
        "subln": gain((L, ATT_DV)),
        "w_out": nrm((L, D_MIX, D_MODEL), D_MIX ** -0.5),
        "ffn2_norm": gain((L, D_MODEL)),
        "ffn2_w_gate": nrm((L, D_MODEL, D_FF), D_MODEL ** -0.5),
        "ffn2_w_up": nrm((L, D_MODEL, D_FF), D_MODEL ** -0.5),
        "ffn2_w_down": nrm((L, D_FF, D_MODEL), D_FF ** -0.5),
    }


def reference(x, ffn1_norm, ffn1_w_gate, ffn1_w_up, ffn1_w_down, mix_norm, w_in,
              conv_w, conv_b, gate_a_w, gate_a_b, gate_x_w, gate_x_b, lru_param,
              rnn_out_norm, q_norm, k_norm, lam_q1, lam_k1, lam_q2, lam_k2, subln,
              w_out, ffn2_norm, ffn2_w_gate, ffn2_w_up, ffn2_w_down):
    bsz, s, _ = x.shape
    cos, sin = rope_tables(s, ATT_DH)
    offs = np.cumsum([0, D_RNN, D_RNN, D_Q, D_Q, D_ATT])

    for l in range(DEPTH):
        h = rms_norm(x, ffn1_norm[l])
        x = x + 0.5 * swiglu(h, ffn1_w_gate[l], ffn1_w_up[l], ffn1_w_down[l])

        h = rms_norm(x, mix_norm[l])
        p = h @ w_in[l]
        x_rnn = p[..., offs[0]:offs[1]]
        g_rnn = p[..., offs[1]:offs[2]]
        q = p[..., offs[2]:offs[3]].reshape(bsz, s, ATT_HEADS, 2, ATT_DH)
        k = p[..., offs[3]:offs[4]].reshape(bsz, s, ATT_HEADS, 2, ATT_DH)
        v = p[..., offs[4]:offs[5]].reshape(bsz, s, ATT_HEADS, ATT_DV)

        xc = causal_depthwise_conv(x_rnn, conv_w[l], conv_b[l])
        y_lru = rg_lru(xc, gate_a_w[l], gate_a_b[l], gate_x_w[l], gate_x_b[l], lru_param[l])
        out_rnn = rms_norm(jax.nn.gelu(g_rnn) * y_lru, rnn_out_norm[l])

        q = apply_rope(rms_norm(q, q_norm[l]), cos, sin)
        k = apply_rope(rms_norm(k, k_norm[l]), cos, sin)
        lam_init = 0.8 - 0.6 * math.exp(-0.3 * l)
        lam = (jnp.exp(jnp.sum(lam_q1[l].astype(jnp.float32) * lam_k1[l].astype(jnp.float32)))
               - jnp.exp(jnp.sum(lam_q2[l].astype(jnp.float32) * lam_k2[l].astype(jnp.float32)))
               + lam_init)
        o = diff_attention(q, k, v, lam)
        o = rms_norm(o, subln[l]) * (1.0 - lam_init)
        out_att = o.reshape(bsz, s, D_ATT)

        mixed = jnp.concatenate([out_rnn, out_att], axis=-1) @ w_out[l]
        x = x + mixed

        h = rms_norm(x, ffn2_norm[l])
        x = x + 0.5 * swiglu(h, ffn2_w_gate[l], ffn2_w_up[l], ffn2_w_down[l])
    return x
```

```python
import functools
import math

import jax
import jax.numpy as jnp
from jax import lax
from jax.experimental import pallas as pl
from jax.experimental.pallas import tpu as pltpu

RNN_BLOCKS = 16
CONV_W = 4
LRU_C = 8.0
ATT_HEADS = 8
ATT_DH = 64
ATT_DV = 2 * ATT_DH
ROPE_THETA = 10000.0
EPS = 1e-6
NEG_INF = -1e30

LANES = 128
SUBLANES = 8
MXU_DIM = 256
VMEM_LIMIT_BYTES = 58 * 1024 * 1024

F32 = jnp.float32
BF16 = jnp.bfloat16


def _rms_scale(x):
    return lax.rsqrt(jnp.mean(x * x, axis=-1, keepdims=True) + EPS)


def _ffn_kernel(x_ref, gain_ref, wg_ref, wu_ref, wd_ref, o_ref, h_ref):
    @pl.when(pl.program_id(1) == 0)
    def _():
        x = x_ref[...]
        h_ref[...] = (x * _rms_scale(x) * gain_ref[...]).astype(BF16)
        o_ref[...] = x

    h = h_ref[...]
    g = jnp.dot(h, wg_ref[...], preferred_element_type=F32)
    u = jnp.dot(h, wu_ref[...], preferred_element_type=F32)
    a = (g * jax.nn.sigmoid(g)) * u * 0.5
    o_ref[...] += jnp.dot(a.astype(BF16), wd_ref[...], preferred_element_type=F32)


def _ffn(x, gain, wg, wu, wd, *, tm, tf):
    m, d = x.shape
    f = wg.shape[1]
    return pl.pallas_call(
        _ffn_kernel,
        grid=(m // tm, f // tf),
        in_specs=[
            pl.BlockSpec((tm, d), lambda i, j: (i, 0)),
            pl.BlockSpec((1, d), lambda i, j: (0, 0)),
            pl.BlockSpec((d, tf), lambda i, j: (0, j)),
            pl.BlockSpec((d, tf), lambda i, j: (0, j)),
            pl.BlockSpec((tf, d), lambda i, j: (j, 0)),
        ],
        out_specs=pl.BlockSpec((tm, d), lambda i, j: (i, 0)),
        out_shape=jax.ShapeDtypeStruct((m, d), F32),
        scratch_shapes=[pltpu.VMEM((tm, d), BF16)],
        compiler_params=pltpu.CompilerParams(
            dimension_semantics=("arbitrary", "arbitrary"),
            vmem_limit_bytes=VMEM_LIMIT_BYTES,
        ),
        name="ffn",
    )(x, gain, wg, wu, wd)


def _mix_in_kernel(x_ref, gain_ref, w_ref, o_ref, h_ref):
    @pl.when(pl.program_id(1) == 0)
    def _():
        x = x_ref[...]
        h_ref[...] = (x * _rms_scale(x) * gain_ref[...]).astype(BF16)

    o_ref[...] = jnp.dot(h_ref[...], w_ref[...], preferred_element_type=F32)


def _mix_in(x, gain, w, *, tm, tn):
    m, d = x.shape
    n = w.shape[1]
    return pl.pallas_call(
        _mix_in_kernel,
        grid=(m // tm, n // tn),
        in_specs=[
            pl.BlockSpec((tm, d), lambda i, j: (i, 0)),
            pl.BlockSpec((1, d), lambda i, j: (0, 0)),
            pl.BlockSpec((d, tn), lambda i, j: (0, j)),
        ],
        out_specs=pl.BlockSpec((tm, tn), lambda i, j: (i, j)),
        out_shape=jax.ShapeDtypeStruct((m, n), F32),
        scratch_shapes=[pltpu.VMEM((tm, d), BF16)],
        compiler_params=pltpu.CompilerParams(
            dimension_semantics=("arbitrary", "arbitrary"),
            vmem_limit_bytes=VMEM_LIMIT_BYTES,
        ),
        name="mix_in",
    )(x, gain, w)


def _rnn_kernel(x_ref, g_ref, cw_ref, cb_ref, wa_ref, ba_ref, wx_ref, bx_ref,
                lru_ref, gain_ref, o_ref, prev_ref, hprev_ref):
    ts, c = x_ref.shape

    @pl.when(pl.program_id(1) == 0)
    def _():
        prev_ref[...] = jnp.zeros_like(prev_ref)
        hprev_ref[...] = jnp.zeros_like(hprev_ref)

    x = x_ref[...]
    xe = jnp.concatenate([prev_ref[...], x], axis=0)
    xc = cb_ref[...] + x * cw_ref[CONV_W - 1:CONV_W, :]
    for k in range(CONV_W - 1):
        shifted = pltpu.roll(xe, CONV_W - 1 - k, axis=0)[SUBLANES:, :]
        xc = xc + shifted * cw_ref[k:k + 1, :]
    prev_ref[...] = x[ts - SUBLANES:, :]

    xcb = xc.astype(BF16)
    n_slabs = c // MXU_DIM

    def gate(w_ref, b_ref):
        parts = [
            jnp.dot(xcb[:, s * MXU_DIM:(s + 1) * MXU_DIM], w_ref[s],
                    preferred_element_type=F32)
            for s in range(n_slabs)
        ]
        return jax.nn.sigmoid(jnp.concatenate(parts, axis=1) + b_ref[...])

    r = gate(wa_ref, ba_ref)
    i = gate(wx_ref, bx_ref)

    neg_l = -lru_ref[...]
    softplus = jnp.maximum(neg_l, 0.0) + jnp.log1p(jnp.exp(-jnp.abs(neg_l)))
    log_a = (-LRU_C * r) * softplus
    a = jnp.exp(log_a)
    mult = jnp.sqrt(-jnp.tanh(log_a) * (1.0 + a * a))
    u = mult * (i * xc)

    row = lax.broadcasted_iota(jnp.int32, (ts, c), 0)
    acc_a, acc_h = a, u
    k = 1
    while k < ts:
        keep = row >= k
        sh_a = pltpu.roll(acc_a, k, axis=0)
        sh_h = pltpu.roll(acc_h, k, axis=0)
        acc_h = jnp.where(keep, acc_a * sh_h + acc_h, acc_h)
        acc_a = jnp.where(keep, acc_a * sh_a, acc_a)
        k *= 2
    h = acc_h + acc_a * hprev_ref[0:1, :]
    hprev_ref[0:1, :] = h[ts - 1:ts, :]

    y = jax.nn.gelu(g_ref[...]) * h
    o_ref[...] = (y * _rms_scale(y) * gain_ref[...]).astype(o_ref.dtype)


def _rnn(p, conv_w, conv_b, wa_bd, ba, wx_bd, bx, lru, gain, *, batch, seq, ts):
    c = conv_w.shape[1]
    nt = seq // ts
    row = lambda b, s: (b * nt + s, 0)
    gate_row = lambda b, s: (b * nt + s, 1)
    const2 = lambda b, s: (0, 0)
    const3 = lambda b, s: (0, 0, 0)
    return pl.pallas_call(
        _rnn_kernel,
        grid=(batch, nt),
        in_specs=[
            pl.BlockSpec((ts, c), row),
            pl.BlockSpec((ts, c), gate_row),
            pl.BlockSpec(conv_w.shape, const2),
            pl.BlockSpec((1, c), const2),
            pl.BlockSpec(wa_bd.shape, const3),
            pl.BlockSpec((1, c), const2),
            pl.BlockSpec(wx_bd.shape, const3),
            pl.BlockSpec((1, c), const2),
            pl.BlockSpec((1, c), const2),
            pl.BlockSpec((1, c), const2),
        ],
        out_specs=pl.BlockSpec((ts, c), row),
        out_shape=jax.ShapeDtypeStruct((batch * seq, c), BF16),
        scratch_shapes=[pltpu.VMEM((SUBLANES, c), F32), pltpu.VMEM((SUBLANES, c), F32)],
        compiler_params=pltpu.CompilerParams(
            dimension_semantics=("arbitrary", "arbitrary"),
            vmem_limit_bytes=VMEM_LIMIT_BYTES,
        ),
        name="rnn",
    )(p, p, conv_w, conv_b, wa_bd, ba, wx_bd, bx, lru, gain)


def _qk_prep(t, gain, cos, sin_signed):
    lane = lax.broadcasted_iota(jnp.int32, t.shape, 1)
    lo = lane < ATT_DH
    t2 = t * t
    ss_lo = jnp.sum(jnp.where(lo, t2, 0.0), axis=-1, keepdims=True)
    ss_hi = jnp.sum(jnp.where(lo, 0.0, t2), axis=-1, keepdims=True)
    ms = jnp.where(lo, ss_lo, ss_hi) * (1.0 / ATT_DH)
    y = t * lax.rsqrt(ms + EPS) * gain
    half = ATT_DH // 2
    ahead = pltpu.roll(y, LANES - half, axis=1)
    behind = pltpu.roll(y, half, axis=1)
    rot = jnp.where((lane & half) == 0, ahead, behind)
    return y * cos + rot * sin_signed


def _attn_kernel(q_ref, k_ref, v_ref, cosq_ref, sinq_ref, cosk_ref, sink_ref,
                 qn_ref, kn_ref, lam_ref, subln_ref, o_ref, kh_ref, vh_ref,
                 *, tk, lam_init):
    tq = q_ref.shape[0]
    i = pl.program_id(2)

    @pl.when(i == 0)
    def _():
        kh_ref[...] = _qk_prep(k_ref[...], kn_ref[...], cosk_ref[...],
                               sink_ref[...]).astype(BF16)
        vh_ref[...] = v_ref[...].astype(BF16)

    qh = _qk_prep(q_ref[...], qn_ref[...], cosq_ref[...], sinq_ref[...])
    qh = qh * (1.0 / math.sqrt(ATT_DH))
    lane = lax.broadcasted_iota(jnp.int32, qh.shape, 1)
    lo = lane < ATT_DH
    q1 = jnp.where(lo, qh, 0.0).astype(BF16)
    q2 = jnp.where(lo, 0.0, qh).astype(BF16)

    nt_dims = (((1,), (1,)), ((), ()))

    def one_map(qm, kb, vb, m, l, acc, mask):
        s = lax.dot_general(qm, kb, nt_dims, preferred_element_type=F32)
        if mask is not None:
            s = jnp.where(mask, s, NEG_INF)
        m_new = jnp.maximum(m, jnp.max(s, axis=-1, keepdims=True))
        alpha = jnp.exp(m - m_new)
        p = jnp.exp(s - m_new)
        l_new = alpha * l + jnp.sum(p, axis=-1, keepdims=True)
        acc_new = alpha * acc + jnp.dot(p.astype(BF16), vb, preferred_element_type=F32)
        return m_new, l_new, acc_new

    def step(j, carry, mask=None):
        start = pl.multiple_of(j * tk, tk)
        kb = kh_ref[pl.ds(start, tk), :]
        vb = vh_ref[pl.ds(start, tk), :]
        m1, l1, a1, m2, l2, a2 = carry
        m1, l1, a1 = one_map(q1, kb, vb, m1, l1, a1, mask)
        m2, l2, a2 = one_map(q2, kb, vb, m2, l2, a2, mask)
        return m1, l1, a1, m2, l2, a2

    m0 = jnp.full((tq, 1), NEG_INF, F32)
    l0 = jnp.zeros((tq, 1), F32)
    a0 = jnp.zeros((tq, ATT_DV), F32)
    carry = lax.fori_loop(0, i, step, (m0, l0, a0, m0, l0, a0))
    diag = (lax.broadcasted_iota(jnp.int32, (tq, tk), 1)
            <= lax.broadcasted_iota(jnp.int32, (tq, tk), 0))
    _, l1, a1, _, l2, a2 = step(i, carry, diag)

    lp = lam_ref[...]
    lam = (jnp.exp(jnp.sum(lp[0:1] * lp[1:2], axis=-1, keepdims=True))
           - jnp.exp(jnp.sum(lp[2:3] * lp[3:4], axis=-1, keepdims=True))
           + lam_init)
    o = a1 / l1 - lam * (a2 / l2)
    o = o * _rms_scale(o) * subln_ref[...] * (1.0 - lam_init)
    o_ref[...] = o.astype(o_ref.dtype)


def _attn(p, cos, sin_signed, qn, kn, lam_params, subln, *, batch, seq, tq, lam_init,
          q_col, k_col, v_col):
    nq = seq // tq
    d = ATT_DV
    const2 = lambda b, h, i: (0, 0)
    return pl.pallas_call(
        functools.partial(_attn_kernel, tk=tq, lam_init=lam_init),
        grid=(batch, ATT_HEADS, nq),
        in_specs=[
            pl.BlockSpec((tq, d), lambda b, h, i: (b * nq + i, q_col + h)),
            pl.BlockSpec((seq, d), lambda b, h, i: (b, k_col + h)),
            pl.BlockSpec((seq, d), lambda b, h, i: (b, v_col + h)),
            pl.BlockSpec((tq, d), lambda b, h, i: (i, 0)),
            pl.BlockSpec((tq, d), lambda b, h, i: (i, 0)),
            pl.BlockSpec((seq, d), const2),
            pl.BlockSpec((seq, d), const2),
            pl.BlockSpec((1, d), const2),
            pl.BlockSpec((1, d), const2),
            pl.BlockSpec(lam_params.shape, const2),
            pl.BlockSpec((1, d), const2),
        ],
        out_specs=pl.BlockSpec((tq, d), lambda b, h, i: (b * nq + i, h)),
        out_shape=jax.ShapeDtypeStruct((batch * seq, ATT_HEADS * d), BF16),
        scratch_shapes=[pltpu.VMEM((seq, d), BF16), pltpu.VMEM((seq, d), BF16)],
        compiler_params=pltpu.CompilerParams(
            dimension_semantics=("arbitrary", "arbitrary", "arbitrary"),
            vmem_limit_bytes=VMEM_LIMIT_BYTES,
        ),
        name="diff_attn",
    )(p, p, p, cos, sin_signed, cos, sin_signed, qn, kn, lam_params, subln)


def _out_proj_kernel(x_ref, rnn_ref, att_ref, w_ref, o_ref):
    c = rnn_ref.shape[1]
    mixed = jnp.dot(rnn_ref[...], w_ref[:c, :], preferred_element_type=F32)
    mixed = mixed + jnp.dot(att_ref[...], w_ref[c:, :], preferred_element_type=F32)
    o_ref[...] = x_ref[...] + mixed


def _out_proj(x, rnn, att, w, *, tm):
    m, d = x.shape
    c = rnn.shape[1]
    return pl.pallas_call(
        _out_proj_kernel,
        grid=(m // tm,),
        in_specs=[
            pl.BlockSpec((tm, d), lambda i: (i, 0)),
            pl.BlockSpec((tm, c), lambda i: (i, 0)),
            pl.BlockSpec((tm, att.shape[1]), lambda i: (i, 0)),
            pl.BlockSpec(w.shape, lambda i: (0, 0)),
        ],
        out_specs=pl.BlockSpec((tm, d), lambda i: (i, 0)),
        out_shape=jax.ShapeDtypeStruct((m, d), F32),
        compiler_params=pltpu.CompilerParams(
            dimension_semantics=("arbitrary",),
            vmem_limit_bytes=VMEM_LIMIT_BYTES,
        ),
        name="out_proj",
    )(x, rnn, att, w)


def _block_diag_slabs(w):
    g, n, _ = w.shape
    per = MXU_DIM // n
    w5 = w.reshape(g // per, per, n, 1, n)
    eye = jnp.eye(per, dtype=bool)[None, :, None, :, None]
    return jnp.where(eye, w5, 0.0).reshape(g // per, MXU_DIM, MXU_DIM)


def _rope_tables(seq):
    inv_freq = ROPE_THETA ** (-jnp.arange(0, ATT_DH, 2, dtype=F32) / ATT_DH)
    ang = jnp.arange(seq, dtype=F32)[:, None] * inv_freq[None, :]
    ang = jnp.concatenate([ang, ang], axis=-1)
    cos, sin = jnp.cos(ang), jnp.sin(ang)
    half = ATT_DH // 2
    sin_signed = jnp.concatenate([-sin[:, :half], sin[:, half:]], axis=-1)
    return jnp.tile(cos, (1, 2)), jnp.tile(sin_signed, (1, 2))


def kernel(x, ffn1_norm, ffn1_w_gate, ffn1_w_up, ffn1_w_down, mix_norm, w_in, conv_w, conv_b, gate_a_w, gate_a_b, gate_x_w, gate_x_b, lru_param, rnn_out_norm, q_norm, k_norm, lam_q1, lam_k1, lam_q2, lam_k2, subln, w_out, ffn2_norm, ffn2_w_gate, ffn2_w_up, ffn2_w_down):
    bsz, seq, d = x.shape
    depth = w_in.shape[0]
    d_rnn = conv_w.shape[-1]
    d_q = ATT_HEADS * 2 * ATT_DH
    assert w_in.shape[-1] == 2 * d_rnn + 2 * d_q + ATT_HEADS * ATT_DV
    assert d_rnn % LANES == 0 and d_q % LANES == 0
    q_col = 2 * d_rnn // ATT_DV
    k_col = q_col + d_q // ATT_DV
    v_col = k_col + d_q // ATT_DV

    cos, sin_signed = _rope_tables(seq)
    xf = x.reshape(bsz * seq, d)
    row = lambda v: v.reshape(1, -1)

    for l in range(depth):
        xf = _ffn(xf, row(ffn1_norm[l]), ffn1_w_gate[l].astype(BF16),
                  ffn1_w_up[l].astype(BF16), ffn1_w_down[l].astype(BF16),
                  tm=1024, tf=256)

        p = _mix_in(xf, row(mix_norm[l]), w_in[l].astype(BF16), tm=1024, tn=512)

        out_rnn = _rnn(
            p, conv_w[l], row(conv_b[l]),
            _block_diag_slabs(gate_a_w[l]).astype(BF16), row(gate_a_b[l]),
            _block_diag_slabs(gate_x_w[l]).astype(BF16), row(gate_x_b[l]),
            row(lru_param[l]), row(rnn_out_norm[l]),
            batch=bsz, seq=seq, ts=512)

        lam_init = 0.8 - 0.6 * math.exp(-0.3 * l)
        lam_params = jnp.stack([lam_q1[l], lam_k1[l], lam_q2[l], lam_k2[l]])
        out_att = _attn(
            p, cos, sin_signed, row(jnp.tile(q_norm[l], 2)), row(jnp.tile(k_norm[l], 2)),
            lam_params, row(subln[l]),
            batch=bsz, seq=seq, tq=256, lam_init=lam_init,
            q_col=q_col, k_col=k_col, v_col=v_col)

        xf = _out_proj(xf, out_rnn, out_att, w_out[l].astype(BF16), tm=512)

        xf = _ffn(xf, row(ffn2_norm[l]), ffn2_w_gate[l].astype(BF16),
                  ffn2_w_up[l].astype(BF16), ffn2_w_down[l].astype(BF16),
                  tm=1024, tf=256)
    return xf.reshape(bsz, seq, d)
```

```python
import functools
import math

import jax
import jax.numpy as jnp
from jax import lax
from jax.experimental import pallas as pl
from jax.experimental.pallas import tpu as pltpu

RNN_BLOCKS = 16
CONV_W = 4
LRU_C = 8.0
ATT_HEADS = 8
ATT_DH = 64
ATT_DV = 2 * ATT_DH
ROPE_THETA = 10000.0
EPS = 1e-6
NEG_INF = -1e30

LANES = 128
SUBLANES = 8
MXU_DIM = 256
VMEM_LIMIT_BYTES = 58 * 1024 * 1024

F32 = jnp.float32
BF16 = jnp.bfloat16


def _rms_scale(x):
    return lax.rsqrt(jnp.mean(x * x, axis=-1, keepdims=True) + EPS)


def _layer_row_spec(layer, width, n_grid):
    zeros = (0,) * 2
    if n_grid == 1:
        return pl.BlockSpec((None, 1, width), lambda i: (layer, *zeros))
    if n_grid == 2:
        return pl.BlockSpec((None, 1, width), lambda i, j: (layer, *zeros))
    return pl.BlockSpec((None, 1, width), lambda i, j, k: (layer, *zeros))


def _ffn_kernel(x_ref, gain_ref, wg_ref, wu_ref, wd_ref, o_ref, h_ref):
    @pl.when(pl.program_id(1) == 0)
    def _():
        x = x_ref[...]
        h_ref[...] = (x * _rms_scale(x) * gain_ref[...]).astype(BF16)
        o_ref[...] = x

    h = h_ref[...]
    g = jnp.dot(h, wg_ref[...].astype(BF16), preferred_element_type=F32)
    u = jnp.dot(h, wu_ref[...].astype(BF16), preferred_element_type=F32)
    a = (g * jax.nn.sigmoid(g)) * u * 0.5
    o_ref[...] += jnp.dot(a.astype(BF16), wd_ref[...].astype(BF16),
                          preferred_element_type=F32)


def _ffn(x, gain, wg, wu, wd, *, layer, tm, tf):
    m, d = x.shape
    f = wg.shape[2]
    return pl.pallas_call(
        _ffn_kernel,
        grid=(m // tm, f // tf),
        in_specs=[
            pl.BlockSpec((tm, d), lambda i, j: (i, 0)),
            _layer_row_spec(layer, d, 2),
            pl.BlockSpec((None, d, tf), lambda i, j: (layer, 0, j)),
            pl.BlockSpec((None, d, tf), lambda i, j: (layer, 0, j)),
            pl.BlockSpec((None, tf, d), lambda i, j: (layer, j, 0)),
        ],
        out_specs=pl.BlockSpec((tm, d), lambda i, j: (i, 0)),
        out_shape=jax.ShapeDtypeStruct((m, d), F32),
        scratch_shapes=[pltpu.VMEM((tm, d), BF16)],
        compiler_params=pltpu.CompilerParams(
            dimension_semantics=("arbitrary", "arbitrary"),
            vmem_limit_bytes=VMEM_LIMIT_BYTES,
        ),
        name="ffn",
    )(x, gain, wg, wu, wd)


def _mix_in_kernel(x_ref, gain_ref, w_ref, o_ref, h_ref):
    @pl.when(pl.program_id(1) == 0)
    def _():
        x = x_ref[...]
        h_ref[...] = (x * _rms_scale(x) * gain_ref[...]).astype(BF16)

    o_ref[...] = jnp.dot(h_ref[...], w_ref[...].astype(BF16),
                         preferred_element_type=F32)


def _mix_in(x, gain, w, *, layer, tm, tn):
    m, d = x.shape
    n = w.shape[2]
    return pl.pallas_call(
        _mix_in_kernel,
        grid=(m // tm, n // tn),
        in_specs=[
            pl.BlockSpec((tm, d), lambda i, j: (i, 0)),
            _layer_row_spec(layer, d, 2),
            pl.BlockSpec((None, d, tn), lambda i, j: (layer, 0, j)),
        ],
        out_specs=pl.BlockSpec((tm, tn), lambda i, j: (i, j)),
        out_shape=jax.ShapeDtypeStruct((m, n), F32),
        scratch_shapes=[pltpu.VMEM((tm, d), BF16)],
        compiler_params=pltpu.CompilerParams(
            dimension_semantics=("arbitrary", "arbitrary"),
            vmem_limit_bytes=VMEM_LIMIT_BYTES,
        ),
        name="mix_in",
    )(x, gain, w)


def _rnn_kernel(x_ref, g_ref, cw_ref, cb_ref, wa_ref, ba_ref, wx_ref, bx_ref,
                lru_ref, gain_ref, o_ref, prev_ref, hprev_ref):
    ts, c = x_ref.shape

    @pl.when(pl.program_id(1) == 0)
    def _():
        prev_ref[...] = jnp.zeros_like(prev_ref)
        hprev_ref[...] = jnp.zeros_like(hprev_ref)

    x = x_ref[...]
    xe = jnp.concatenate([prev_ref[...], x], axis=0)
    xc = cb_ref[...] + x * cw_ref[CONV_W - 1:CONV_W, :]
    for k in range(CONV_W - 1):
        shifted = pltpu.roll(xe, CONV_W - 1 - k, axis=0)[SUBLANES:, :]
        xc = xc + shifted * cw_ref[k:k + 1, :]
    prev_ref[...] = x[ts - SUBLANES:, :]

    xcb = xc.astype(BF16)
    n_slabs = c // MXU_DIM

    def gate(w_ref, b_ref):
        parts = [
            jnp.dot(xcb[:, s * MXU_DIM:(s + 1) * MXU_DIM], w_ref[s],
                    preferred_element_type=F32)
            for s in range(n_slabs)
        ]
        return jax.nn.sigmoid(jnp.concatenate(parts, axis=1) + b_ref[...])

    r = gate(wa_ref, ba_ref)
    i = gate(wx_ref, bx_ref)

    neg_l = -lru_ref[...]
    softplus = jnp.maximum(neg_l, 0.0) + jnp.log1p(jnp.exp(-jnp.abs(neg_l)))
    log_a = (-LRU_C * r) * softplus
    a = jnp.exp(log_a)
    mult = jnp.sqrt(-jnp.tanh(log_a) * (1.0 + a * a))
    u = mult * (i * xc)

    row = lax.broadcasted_iota(jnp.int32, (ts, c), 0)
    acc_a, acc_h = a, u
    k = 1
    while k < ts:
        keep = row >= k
        sh_a = pltpu.roll(acc_a, k, axis=0)
        sh_h = pltpu.roll(acc_h, k, axis=0)
        acc_h = jnp.where(keep, acc_a * sh_h + acc_h, acc_h)
        acc_a = jnp.where(keep, acc_a * sh_a, acc_a)
        k *= 2
    h = acc_h + acc_a * hprev_ref[0:1, :]
    hprev_ref[0:1, :] = h[ts - 1:ts, :]

    y = jax.nn.gelu(g_ref[...]) * h
    o_ref[...] = (y * _rms_scale(y) * gain_ref[...]).astype(o_ref.dtype)


def _rnn(p, conv_w, conv_b, wa_bd, ba, wx_bd, bx, lru, gain, *, layer, batch, seq, ts):
    c = conv_w.shape[2]
    nt = seq // ts
    row = lambda b, s: (b * nt + s, 0)
    gate_row = lambda b, s: (b * nt + s, 1)
    vec = _layer_row_spec(layer, c, 2)
    return pl.pallas_call(
        _rnn_kernel,
        grid=(batch, nt),
        in_specs=[
            pl.BlockSpec((ts, c), row),
            pl.BlockSpec((ts, c), gate_row),
            pl.BlockSpec((None, CONV_W, c), lambda b, s: (layer, 0, 0)),
            vec,
            pl.BlockSpec(wa_bd.shape, lambda b, s: (0, 0, 0)),
            vec,
            pl.BlockSpec(wx_bd.shape, lambda b, s: (0, 0, 0)),
            vec,
            vec,
            vec,
        ],
        out_specs=pl.BlockSpec((ts, c), row),
        out_shape=jax.ShapeDtypeStruct((batch * seq, c), BF16),
        scratch_shapes=[pltpu.VMEM((SUBLANES, c), F32), pltpu.VMEM((SUBLANES, c), F32)],
        compiler_params=pltpu.CompilerParams(
            dimension_semantics=("arbitrary", "arbitrary"),
            vmem_limit_bytes=VMEM_LIMIT_BYTES,
        ),
        name="rnn",
    )(p, p, conv_w, conv_b, wa_bd, ba, wx_bd, bx, lru, gain)


def _qk_prep(t, gain, cos, sin_signed):
    lane = lax.broadcasted_iota(jnp.int32, t.shape, 1)
    lo = lane < ATT_DH
    t2 = t * t
    ss_lo = jnp.sum(jnp.where(lo, t2, 0.0), axis=-1, keepdims=True)
    ss_hi = jnp.sum(jnp.where(lo, 0.0, t2), axis=-1, keepdims=True)
    ms = jnp.where(lo, ss_lo, ss_hi) * (1.0 / ATT_DH)
    y = t * lax.rsqrt(ms + EPS) * gain
    half = ATT_DH // 2
    ahead = pltpu.roll(y, LANES - half, axis=1)
    behind = pltpu.roll(y, half, axis=1)
    rot = jnp.where((lane & half) == 0, ahead, behind)
    return y * cos + rot * sin_signed


def _attn_kernel(q_ref, k_ref, v_ref, cos_ref, sin_ref, qn_ref, kn_ref, lam_ref,
                 subln_ref, o_ref, kh_ref, vh_ref, *, tq, lam_init):
    seq = q_ref.shape[0]
    kh_ref[...] = _qk_prep(k_ref[...], kn_ref[...], cos_ref[...], sin_ref[...]).astype(BF16)
    vh_ref[...] = v_ref[...].astype(BF16)

    lp = lam_ref[...]
    lam = (jnp.exp(jnp.sum(lp[0:1] * lp[1:2], axis=-1, keepdims=True))
           - jnp.exp(jnp.sum(lp[2:3] * lp[3:4], axis=-1, keepdims=True))
           + lam_init)

    lane = lax.broadcasted_iota(jnp.int32, (tq, ATT_DV), 1)
    lo = lane < ATT_DH
    causal = (lax.broadcasted_iota(jnp.int32, (tq, tq), 1)
              <= lax.broadcasted_iota(jnp.int32, (tq, tq), 0))
    nt_dims = (((1,), (1,)), ((), ()))

    for blk in range(seq // tq):
        r0, r1 = blk * tq, (blk + 1) * tq
        rows = slice(r0, r1)
        qh = _qk_prep(q_ref[rows, :], qn_ref[...], cos_ref[rows, :], sin_ref[rows, :])
        qh = qh * (1.0 / math.sqrt(ATT_DH))
        kb = kh_ref[0:r1, :]

        def softmax_terms(qm):
            s = lax.dot_general(qm, kb, nt_dims, preferred_element_type=F32)
            s_diag = jnp.where(causal, s[:, r0:r1], NEG_INF)
            m = jnp.max(s_diag, axis=-1, keepdims=True)
            if blk:
                s_past = s[:, 0:r0]
                m = jnp.maximum(m, jnp.max(s_past, axis=-1, keepdims=True))
            p_diag = jnp.exp(s_diag - m)
            l = jnp.sum(p_diag, axis=-1, keepdims=True)
            p_past = None
            if blk:
                p_past = jnp.exp(s_past - m)
                l = l + jnp.sum(p_past, axis=-1, keepdims=True)
            return p_past, p_diag, l

        pp1, pd1, l1 = softmax_terms(jnp.where(lo, qh, 0.0).astype(BF16))
        pp2, pd2, l2 = softmax_terms(jnp.where(lo, 0.0, qh).astype(BF16))
        c1 = 1.0 / l1
        c2 = lam / l2
        w_diag = (pd1 * c1 - pd2 * c2).astype(BF16)
        o = jnp.dot(w_diag, vh_ref[rows, :], preferred_element_type=F32)
        if blk:
            w_past = (pp1 * c1 - pp2 * c2).astype(BF16)
            o = o + jnp.dot(w_past, vh_ref[0:r0, :], preferred_element_type=F32)
        o = o * _rms_scale(o) * subln_ref[...] * (1.0 - lam_init)
        o_ref[rows, :] = o.astype(o_ref.dtype)


def _attn(p, cos, sin_signed, qn, kn, lam_params, subln, *, layer, batch, seq, tq,
          lam_init, q_col, k_col, v_col):
    d = ATT_DV
    const2 = lambda b, h: (0, 0)
    vec = _layer_row_spec(layer, d, 2)
    return pl.pallas_call(
        functools.partial(_attn_kernel, tq=tq, lam_init=lam_init),
        grid=(batch, ATT_HEADS),
        in_specs=[
            pl.BlockSpec((seq, d), lambda b, h: (b, q_col + h)),
            pl.BlockSpec((seq, d), lambda b, h: (b, k_col + h)),
            pl.BlockSpec((seq, d), lambda b, h: (b, v_col + h)),
            pl.BlockSpec((seq, d), const2),
            pl.BlockSpec((seq, d), const2),
            vec,
            vec,
            pl.BlockSpec((None, 4, ATT_DH), lambda b, h: (layer, 0, 0)),
            vec,
        ],
        out_specs=pl.BlockSpec((seq, d), lambda b, h: (b, h)),
        out_shape=jax.ShapeDtypeStruct((batch * seq, ATT_HEADS * d), BF16),
        scratch_shapes=[pltpu.VMEM((seq, d), BF16), pltpu.VMEM((seq, d), BF16)],
        compiler_params=pltpu.CompilerParams(
            dimension_semantics=("arbitrary", "arbitrary"),
            vmem_limit_bytes=VMEM_LIMIT_BYTES,
        ),
        name="diff_attn",
    )(p, p, p, cos, sin_signed, qn, kn, lam_params, subln)


def _out_proj_kernel(x_ref, rnn_ref, att_ref, w_ref, o_ref, wb_ref):
    @pl.when(pl.program_id(0) == 0)
    def _():
        wb_ref[...] = w_ref[...].astype(BF16)

    c = rnn_ref.shape[1]
    mixed = jnp.dot(rnn_ref[...], wb_ref[:c, :], preferred_element_type=F32)
    mixed = mixed + jnp.dot(att_ref[...], wb_ref[c:, :], preferred_element_type=F32)
    o_ref[...] = x_ref[...] + mixed


def _out_proj(x, rnn, att, w, *, layer, tm):
    m, d = x.shape
    c = rnn.shape[1]
    k = w.shape[1]
    return pl.pallas_call(
        _out_proj_kernel,
        grid=(m // tm,),
        in_specs=[
            pl.BlockSpec((tm, d), lambda i: (i, 0)),
            pl.BlockSpec((tm, c), lambda i: (i, 0)),
            pl.BlockSpec((tm, att.shape[1]), lambda i: (i, 0)),
            pl.BlockSpec((None, k, d), lambda i: (layer, 0, 0),
                         pipeline_mode=pl.Buffered(1)),
        ],
        out_specs=pl.BlockSpec((tm, d), lambda i: (i, 0)),
        out_shape=jax.ShapeDtypeStruct((m, d), F32),
        scratch_shapes=[pltpu.VMEM((k, d), BF16)],
        compiler_params=pltpu.CompilerParams(
            dimension_semantics=("arbitrary",),
            vmem_limit_bytes=VMEM_LIMIT_BYTES,
        ),
        name="out_proj",
    )(x, rnn, att, w)


def _block_diag_slabs(w):
    g, n, _ = w.shape
    per = MXU_DIM // n
    w5 = w.reshape(g // per, per, n, 1, n)
    eye = jnp.eye(per, dtype=bool)[None, :, None, :, None]
    return jnp.where(eye, w5, 0.0).reshape(g // per, MXU_DIM, MXU_DIM)


def _rope_tables(seq):
    inv_freq = ROPE_THETA ** (-jnp.arange(0, ATT_DH, 2, dtype=F32) / ATT_DH)
    ang = jnp.arange(seq, dtype=F32)[:, None] * inv_freq[None, :]
    ang = jnp.concatenate([ang, ang], axis=-1)
    cos, sin = jnp.cos(ang), jnp.sin(ang)
    half = ATT_DH // 2
    sin_signed = jnp.concatenate([-sin[:, :half], sin[:, half:]], axis=-1)
    return jnp.tile(cos, (1, 2)), jnp.tile(sin_signed, (1, 2))


def kernel(x, ffn1_norm, ffn1_w_gate, ffn1_w_up, ffn1_w_down, mix_norm, w_in, conv_w, conv_b, gate_a_w, gate_a_b, gate_x_w, gate_x_b, lru_param, rnn_out_norm, q_norm, k_norm, lam_q1, lam_k1, lam_q2, lam_k2, subln, w_out, ffn2_norm, ffn2_w_gate, ffn2_w_up, ffn2_w_down):
    bsz, seq, d = x.shape
    depth = w_in.shape[0]
    d_rnn = conv_w.shape[-1]
    d_q = ATT_HEADS * 2 * ATT_DH
    assert w_in.shape[-1] == 2 * d_rnn + 2 * d_q + ATT_HEADS * ATT_DV
    assert d_rnn % LANES == 0 and d_q % LANES == 0
    q_col = 2 * d_rnn // ATT_DV
    k_col = q_col + d_q // ATT_DV
    v_col = k_col + d_q // ATT_DV

    cos, sin_signed = _rope_tables(seq)
    xf = x.reshape(bsz * seq, d)
    rows = lambda v: v.reshape(depth, 1, -1)
    ffn1_norm, ffn2_norm, mix_norm = rows(ffn1_norm), rows(ffn2_norm), rows(mix_norm)
    conv_b, lru_param, rnn_out_norm = rows(conv_b), rows(lru_param), rows(rnn_out_norm)
    gate_a_b, gate_x_b, subln = rows(gate_a_b), rows(gate_x_b), rows(subln)
    q_gain = rows(jnp.tile(q_norm, (1, 2)))
    k_gain = rows(jnp.tile(k_norm, (1, 2)))
    lam_params = jnp.stack([lam_q1, lam_k1, lam_q2, lam_k2], axis=1)

    for l in range(depth):
        xf = _ffn(xf, ffn1_norm, ffn1_w_gate, ffn1_w_up, ffn1_w_down,
                  layer=l, tm=1024, tf=256)

        p = _mix_in(xf, mix_norm, w_in, layer=l, tm=1024, tn=512)

        out_rnn = _rnn(
            p, conv_w, conv_b,
            _block_diag_slabs(gate_a_w[l]).astype(BF16), gate_a_b,
            _block_diag_slabs(gate_x_w[l]).astype(BF16), gate_x_b,
            lru_param, rnn_out_norm, layer=l, batch=bsz, seq=seq, ts=512)

        lam_init = 0.8 - 0.6 * math.exp(-0.3 * l)
        out_att = _attn(
            p, cos, sin_signed, q_gain, k_gain, lam_params, subln,
            layer=l, batch=bsz, seq=seq, tq=256, lam_init=lam_init,
            q_col=q_col, k_col=k_col, v_col=v_col)

        xf = _out_proj(xf, out_rnn, out_att, w_out, layer=l, tm=512)

        xf = _ffn(xf, ffn2_norm, ffn2_w_gate, ffn2_w_up, ffn2_w_down,
                  layer=l, tm=1024, tf=256)
    return xf.reshape(bsz, seq, d)
```

```python
import functools
import math

import jax
import jax.numpy as jnp
from jax import lax
from jax.experimental import pallas as pl
from jax.experimental.pallas import tpu as pltpu

RNN_BLOCKS = 16
CONV_W = 4
LRU_C = 8.0
ATT_HEADS = 8
ATT_DH = 64
ATT_DV = 2 * ATT_DH
ROPE_THETA = 10000.0
EPS = 1e-6
NEG_INF = -1e30

LANES = 128
SUBLANES = 8
BF16_SUBLANES = 16
MXU_DIM = 256
VMEM_LIMIT_BYTES = 58 * 1024 * 1024

F32 = jnp.float32
BF16 = jnp.bfloat16


def _rms_scale(x):
    return lax.rsqrt(jnp.mean(x * x, axis=-1, keepdims=True) + EPS)


def _layer_row_spec(layer, width, n_grid):
    zeros = (0,) * 2
    if n_grid == 1:
        return pl.BlockSpec((None, 1, width), lambda i: (layer, *zeros))
    if n_grid == 2:
        return pl.BlockSpec((None, 1, width), lambda i, j: (layer, *zeros))
    return pl.BlockSpec((None, 1, width), lambda i, j, k: (layer, *zeros))


def _ffn_kernel(x_ref, gain_ref, wg_ref, wu_ref, wd_ref, o_ref, h_ref):
    @pl.when(pl.program_id(1) == 0)
    def _():
        x = x_ref[...]
        h_ref[...] = (x * _rms_scale(x) * gain_ref[...]).astype(BF16)
        o_ref[...] = x

    h = h_ref[...]
    g = jnp.dot(h, wg_ref[...].astype(BF16), preferred_element_type=F32)
    u = jnp.dot(h, wu_ref[...].astype(BF16), preferred_element_type=F32)
    a = (g * jax.nn.sigmoid(g)) * u * 0.5
    o_ref[...] += jnp.dot(a.astype(BF16), wd_ref[...].astype(BF16),
                          preferred_element_type=F32)


def _ffn(x, gain, wg, wu, wd, *, layer, tm, tf):
    m, d = x.shape
    f = wg.shape[2]
    return pl.pallas_call(
        _ffn_kernel,
        grid=(m // tm, f // tf),
        in_specs=[
            pl.BlockSpec((tm, d), lambda i, j: (i, 0)),
            _layer_row_spec(layer, d, 2),
            pl.BlockSpec((None, d, tf), lambda i, j: (layer, 0, j)),
            pl.BlockSpec((None, d, tf), lambda i, j: (layer, 0, j)),
            pl.BlockSpec((None, tf, d), lambda i, j: (layer, j, 0)),
        ],
        out_specs=pl.BlockSpec((tm, d), lambda i, j: (i, 0)),
        out_shape=jax.ShapeDtypeStruct((m, d), F32),
        scratch_shapes=[pltpu.VMEM((tm, d), BF16)],
        compiler_params=pltpu.CompilerParams(
            dimension_semantics=("arbitrary", "arbitrary"),
            vmem_limit_bytes=VMEM_LIMIT_BYTES,
        ),
        name="ffn",
    )(x, gain, wg, wu, wd)


def _mix_in_kernel(x_ref, gain_ref, w_ref, o_ref, h_ref):
    @pl.when(pl.program_id(1) == 0)
    def _():
        x = x_ref[...]
        h_ref[...] = (x * _rms_scale(x) * gain_ref[...]).astype(BF16)

    o_ref[...] = jnp.dot(h_ref[...], w_ref[...].astype(BF16),
                         preferred_element_type=F32)


def _mix_in(x, gain, w, *, layer, tm, tn):
    m, d = x.shape
    n = w.shape[2]
    return pl.pallas_call(
        _mix_in_kernel,
        grid=(m // tm, n // tn),
        in_specs=[
            pl.BlockSpec((tm, d), lambda i, j: (i, 0)),
            _layer_row_spec(layer, d, 2),
            pl.BlockSpec((None, d, tn), lambda i, j: (layer, 0, j)),
        ],
        out_specs=pl.BlockSpec((tm, tn), lambda i, j: (i, j)),
        out_shape=jax.ShapeDtypeStruct((m, n), F32),
        scratch_shapes=[pltpu.VMEM((tm, d), BF16)],
        compiler_params=pltpu.CompilerParams(
            dimension_semantics=("arbitrary", "arbitrary"),
            vmem_limit_bytes=VMEM_LIMIT_BYTES,
        ),
        name="mix_in",
    )(x, gain, w)


def _rnn_kernel(x_ref, g_ref, cw_ref, cb_ref, wa_ref, ba_ref, wx_ref, bx_ref,
                lru_ref, gain_ref, o_ref, xs_ref, hprev_ref):
    ts, c = x_ref.shape

    @pl.when(pl.program_id(1) == 0)
    def _():
        xs_ref[0:SUBLANES, :] = jnp.zeros((SUBLANES, c), F32)
        hprev_ref[...] = jnp.zeros_like(hprev_ref)

    x = x_ref[...]
    xs_ref[SUBLANES:, :] = x
    xc = cb_ref[...] + x * cw_ref[CONV_W - 1:CONV_W, :]
    for k in range(CONV_W - 1):
        start = SUBLANES - (CONV_W - 1 - k)
        xc = xc + xs_ref[start:start + ts, :] * cw_ref[k:k + 1, :]
    xs_ref[0:SUBLANES, :] = x[ts - SUBLANES:, :]

    xcb = xc.astype(BF16)
    n_slabs = c // MXU_DIM

    def gate(w_ref, b_ref):
        parts = [
            jnp.dot(xcb[:, s * MXU_DIM:(s + 1) * MXU_DIM], w_ref[s],
                    preferred_element_type=F32)
            for s in range(n_slabs)
        ]
        return jax.nn.sigmoid(jnp.concatenate(parts, axis=1) + b_ref[...])

    r = gate(wa_ref, ba_ref)
    i = gate(wx_ref, bx_ref)

    neg_l = -lru_ref[...]
    softplus = jnp.maximum(neg_l, 0.0) + jnp.log1p(jnp.exp(-jnp.abs(neg_l)))
    log_a = (-LRU_C * r) * softplus
    a = jnp.exp(log_a)
    mult = jnp.sqrt(-jnp.tanh(log_a) * (1.0 + a * a))
    u = mult * (i * xc)

    sub = lax.broadcasted_iota(jnp.int32, (SUBLANES, c), 0)
    carry = hprev_ref[0:1, :]
    groups = []
    for g in range(ts // SUBLANES):
        rows = slice(g * SUBLANES, (g + 1) * SUBLANES)
        acc_a, acc_h = a[rows, :], u[rows, :]
        k = 1
        while k < SUBLANES:
            keep = sub >= k
            sh_a = pltpu.roll(acc_a, k, axis=0)
            sh_h = pltpu.roll(acc_h, k, axis=0)
            acc_h = jnp.where(keep, acc_a * sh_h + acc_h, acc_h)
            acc_a = jnp.where(keep, acc_a * sh_a, acc_a)
            k *= 2
        h_g = acc_h + acc_a * carry
        carry = h_g[SUBLANES - 1:SUBLANES, :]
        groups.append(h_g)
    h = jnp.concatenate(groups, axis=0)
    hprev_ref[0:1, :] = carry

    y = jax.nn.gelu(g_ref[...]) * h
    o_ref[...] = (y * _rms_scale(y) * gain_ref[...]).astype(o_ref.dtype)


def _rnn(p, conv_w, conv_b, wa_bd, ba, wx_bd, bx, lru, gain, *, layer, batch, seq, ts):
    c = conv_w.shape[2]
    nt = seq // ts
    row = lambda b, s: (b * nt + s, 0)
    gate_row = lambda b, s: (b * nt + s, 1)
    vec = _layer_row_spec(layer, c, 2)
    return pl.pallas_call(
        _rnn_kernel,
        grid=(batch, nt),
        in_specs=[
            pl.BlockSpec((ts, c), row),
            pl.BlockSpec((ts, c), gate_row),
            pl.BlockSpec((None, CONV_W, c), lambda b, s: (layer, 0, 0)),
            vec,
            pl.BlockSpec(wa_bd.shape, lambda b, s: (0, 0, 0)),
            vec,
            pl.BlockSpec(wx_bd.shape, lambda b, s: (0, 0, 0)),
            vec,
            vec,
            vec,
        ],
        out_specs=pl.BlockSpec((ts, c), row),
        out_shape=jax.ShapeDtypeStruct((batch * seq, c), BF16),
        scratch_shapes=[pltpu.VMEM((SUBLANES + ts, c), F32), pltpu.VMEM((SUBLANES, c), F32)],
        compiler_params=pltpu.CompilerParams(
            dimension_semantics=("arbitrary", "arbitrary"),
            vmem_limit_bytes=VMEM_LIMIT_BYTES,
        ),
        name="rnn",
    )(p, p, conv_w, conv_b, wa_bd, ba, wx_bd, bx, lru, gain)


def _qk_prep(t, gain, cos, sin_signed):
    lane = lax.broadcasted_iota(jnp.int32, t.shape, 1)
    lo = lane < ATT_DH
    t2 = t * t
    ss_lo = jnp.sum(jnp.where(lo, t2, 0.0), axis=-1, keepdims=True)
    ss_hi = jnp.sum(jnp.where(lo, 0.0, t2), axis=-1, keepdims=True)
    ms = jnp.where(lo, ss_lo, ss_hi) * (1.0 / ATT_DH)
    y = t * lax.rsqrt(ms + EPS) * gain
    half = ATT_DH // 2
    ahead = pltpu.roll(y, LANES - half, axis=1)
    behind = pltpu.roll(y, half, axis=1)
    rot = jnp.where((lane & half) == 0, ahead, behind)
    return y * cos + rot * sin_signed


def _attn_kernel(q_ref, k_ref, v_ref, cos_ref, sin_ref, qn_ref, kn_ref, lam_ref,
                 subln_ref, o_ref, kh_ref, vt_ref, *, tq, lam_init):
    seq = q_ref.shape[0]
    kh_ref[...] = _qk_prep(k_ref[...], kn_ref[...], cos_ref[...], sin_ref[...]).astype(BF16)
    vt_ref[0:ATT_DV, :] = v_ref[...].T.astype(BF16)
    vt_ref[ATT_DV:, :] = jnp.ones((vt_ref.shape[0] - ATT_DV, seq), BF16)

    lp = lam_ref[...]
    lam = (jnp.exp(jnp.sum(lp[0:1] * lp[1:2], axis=-1, keepdims=True))
           - jnp.exp(jnp.sum(lp[2:3] * lp[3:4], axis=-1, keepdims=True))
           + lam_init)

    lane = lax.broadcasted_iota(jnp.int32, (tq, ATT_DV), 1)
    lo = lane < ATT_DH
    causal_t = (lax.broadcasted_iota(jnp.int32, (tq, 2 * tq), 0)
                <= (lax.broadcasted_iota(jnp.int32, (tq, 2 * tq), 1) & (tq - 1)))
    nt_dims = (((1,), (1,)), ((), ()))

    def score_phase(blk, out):
        rows = slice(blk * tq, (blk + 1) * tq)
        qh = _qk_prep(q_ref[rows, :], qn_ref[...], cos_ref[rows, :], sin_ref[rows, :])
        qh = qh * (math.log2(math.e) / math.sqrt(ATT_DH))
        q_both = jnp.concatenate([jnp.where(lo, qh, 0.0), jnp.where(lo, 0.0, qh)],
                                 axis=0).astype(BF16)
        chunks, m = [], None
        for c in range(blk + 1):
            st = lax.dot_general(kh_ref[c * tq:(c + 1) * tq, :], q_both, nt_dims,
                                 preferred_element_type=F32)
            if c == blk:
                st = jnp.where(causal_t, st, NEG_INF)
            cm = jnp.max(st, axis=0, keepdims=True)
            m = cm if m is None else jnp.maximum(m, cm)
            chunks.append(st)
            yield
        out.append((chunks, m))

    def value_phase(blk, chunks, m):
        acc = None
        for c, st in enumerate(chunks):
            part = jnp.dot(vt_ref[:, c * tq:(c + 1) * tq], jnp.exp2(st - m).astype(BF16),
                           preferred_element_type=F32)
            acc = part if acc is None else acc + part
            yield
        l = acc[ATT_DV:ATT_DV + 1, :]
        a1, a2 = acc[0:ATT_DV, 0:tq], acc[0:ATT_DV, tq:]
        l1, l2 = l[:, 0:tq], l[:, tq:]
        ot = a1 * (1.0 / l1) - a2 * (lam / l2)
        ot = ot * lax.rsqrt(jnp.mean(ot * ot, axis=0, keepdims=True) + EPS)
        o = ot.T * (subln_ref[...] * (1.0 - lam_init))
        o_ref[blk * tq:(blk + 1) * tq, :] = o.astype(o_ref.dtype)

    n_blk = seq // tq
    scored = []
    for _ in score_phase(0, scored):
        pass
    for blk in range(n_blk):
        phases = [value_phase(blk, *scored[blk])]
        if blk + 1 < n_blk:
            phases.append(score_phase(blk + 1, scored))
        while phases:
            for ph in list(phases):
                if next(ph, StopIteration) is StopIteration:
                    phases.remove(ph)


def _attn(p, cos, sin_signed, qn, kn, lam_params, subln, *, layer, batch, seq, tq,
          lam_init, q_col, k_col, v_col):
    d = ATT_DV
    const2 = lambda b, h: (0, 0)
    vec = _layer_row_spec(layer, d, 2)
    return pl.pallas_call(
        functools.partial(_attn_kernel, tq=tq, lam_init=lam_init),
        grid=(batch, ATT_HEADS),
        in_specs=[
            pl.BlockSpec((seq, d), lambda b, h: (b, q_col + h)),
            pl.BlockSpec((seq, d), lambda b, h: (b, k_col + h)),
            pl.BlockSpec((seq, d), lambda b, h: (b, v_col + h)),
            pl.BlockSpec((seq, d), const2),
            pl.BlockSpec((seq, d), const2),
            vec,
            vec,
            pl.BlockSpec((None, 4, ATT_DH), lambda b, h: (layer, 0, 0)),
            vec,
        ],
        out_specs=pl.BlockSpec((seq, d), lambda b, h: (b, h)),
        out_shape=jax.ShapeDtypeStruct((batch * seq, ATT_HEADS * d), BF16),
        scratch_shapes=[pltpu.VMEM((seq, d), BF16), pltpu.VMEM((d + BF16_SUBLANES, seq), BF16)],
        compiler_params=pltpu.CompilerParams(
            dimension_semantics=("arbitrary", "arbitrary"),
            vmem_limit_bytes=VMEM_LIMIT_BYTES,
        ),
        name="diff_attn",
    )(p, p, p, cos, sin_signed, qn, kn, lam_params, subln)


def _out_proj_kernel(x_ref, rnn_ref, att_ref, w_ref, o_ref, wb_ref):
    @pl.when(pl.program_id(0) == 0)
    def _():
        wb_ref[...] = w_ref[...].astype(BF16)

    c = rnn_ref.shape[1]
    mixed = jnp.dot(rnn_ref[...], wb_ref[:c, :], preferred_element_type=F32)
    mixed = mixed + jnp.dot(att_ref[...], wb_ref[c:, :], preferred_element_type=F32)
    o_ref[...] = x_ref[...] + mixed


def _out_proj(x, rnn, att, w, *, layer, tm):
    m, d = x.shape
    c = rnn.shape[1]
    k = w.shape[1]
    return pl.pallas_call(
        _out_proj_kernel,
        grid=(m // tm,),
        in_specs=[
            pl.BlockSpec((tm, d), lambda i: (i, 0)),
            pl.BlockSpec((tm, c), lambda i: (i, 0)),
            pl.BlockSpec((tm, att.shape[1]), lambda i: (i, 0)),
            pl.BlockSpec((None, k, d), lambda i: (layer, 0, 0),
                         pipeline_mode=pl.Buffered(1)),
        ],
        out_specs=pl.BlockSpec((tm, d), lambda i: (i, 0)),
        out_shape=jax.ShapeDtypeStruct((m, d), F32),
        scratch_shapes=[pltpu.VMEM((k, d), BF16)],
        compiler_params=pltpu.CompilerParams(
            dimension_semantics=("arbitrary",),
            vmem_limit_bytes=VMEM_LIMIT_BYTES,
        ),
        name="out_proj",
    )(x, rnn, att, w)


def _block_diag_slabs(w):
    g, n, _ = w.shape
    per = MXU_DIM // n
    w5 = w.reshape(g // per, per, n, 1, n)
    eye = jnp.eye(per, dtype=bool)[None, :, None, :, None]
    return jnp.where(eye, w5, 0.0).reshape(g // per, MXU_DIM, MXU_DIM)


def _rope_tables(seq):
    inv_freq = ROPE_THETA ** (-jnp.arange(0, ATT_DH, 2, dtype=F32) / ATT_DH)
    ang = jnp.arange(seq, dtype=F32)[:, None] * inv_freq[None, :]
    ang = jnp.concatenate([ang, ang], axis=-1)
    cos, sin = jnp.cos(ang), jnp.sin(ang)
    half = ATT_DH // 2
    sin_signed = jnp.concatenate([-sin[:, :half], sin[:, half:]], axis=-1)
    return jnp.tile(cos, (1, 2)), jnp.tile(sin_signed, (1, 2))


def kernel(x, ffn1_norm, ffn1_w_gate, ffn1_w_up, ffn1_w_down, mix_norm, w_in, conv_w, conv_b, gate_a_w, gate_a_b, gate_x_w, gate_x_b, lru_param, rnn_out_norm, q_norm, k_norm, lam_q1, lam_k1, lam_q2, lam_k2, subln, w_out, ffn2_norm, ffn2_w_gate, ffn2_w_up, ffn2_w_down):
    bsz, seq, d = x.shape
    depth = w_in.shape[0]
    d_rnn = conv_w.shape[-1]
    d_q = ATT_HEADS * 2 * ATT_DH
    assert w_in.shape[-1] == 2 * d_rnn + 2 * d_q + ATT_HEADS * ATT_DV
    assert d_rnn % LANES == 0 and d_q % LANES == 0
    q_col = 2 * d_rnn // ATT_DV
    k_col = q_col + d_q // ATT_DV
    v_col = k_col + d_q // ATT_DV

    cos, sin_signed = _rope_tables(seq)
    xf = x.reshape(bsz * seq, d)
    rows = lambda v: v.reshape(depth, 1, -1)
    ffn1_norm, ffn2_norm, mix_norm = rows(ffn1_norm), rows(ffn2_norm), rows(mix_norm)
    conv_b, lru_param, rnn_out_norm = rows(conv_b), rows(lru_param), rows(rnn_out_norm)
    gate_a_b, gate_x_b, subln = rows(gate_a_b), rows(gate_x_b), rows(subln)
    q_gain = rows(jnp.tile(q_norm, (1, 2)))
    k_gain = rows(jnp.tile(k_norm, (1, 2)))
    lam_params = jnp.stack([lam_q1, lam_k1, lam_q2, lam_k2], axis=1)

    for l in range(depth):
        xf = _ffn(xf, ffn1_norm, ffn1_w_gate, ffn1_w_up, ffn1_w_down,
                  layer=l, tm=1024, tf=256)

        p = _mix_in(xf, mix_norm, w_in, layer=l, tm=1024, tn=1024)

        out_rnn = _rnn(
            p, conv_w, conv_b,
            _block_diag_slabs(gate_a_w[l]).astype(BF16), gate_a_b,
            _block_diag_slabs(gate_x_w[l]).astype(BF16), gate_x_b,
            lru_param, rnn_out_norm, layer=l, batch=bsz, seq=seq, ts=512)

        lam_init = 0.8 - 0.6 * math.exp(-0.3 * l)
        out_att = _attn(
            p, cos, sin_signed, q_gain, k_gain, lam_params, subln,
            layer=l, batch=bsz, seq=seq, tq=256, lam_init=lam_init,
            q_col=q_col, k_col=k_col, v_col=v_col)

        xf = _out_proj(xf, out_rnn, out_att, w_out, layer=l, tm=512)

        xf = _ffn(xf, ffn2_norm, ffn2_w_gate, ffn2_w_up, ffn2_w_down,
                  layer=l, tm=1024, tf=256)
    return xf.reshape(bsz, seq, d)
```

```python
import functools
import math

import jax
import jax.numpy as jnp
from jax import lax
from jax.experimental import pallas as pl
from jax.experimental.pallas import tpu as pltpu

RNN_BLOCKS = 16
CONV_W = 4
LRU_C = 8.0
ATT_HEADS = 8
ATT_DH = 64
ATT_DV = 2 * ATT_DH
ROPE_THETA = 10000.0
EPS = 1e-6
NEG_INF = -1e30

LANES = 128
SUBLANES = 8
BF16_SUBLANES = 16
MXU_DIM = 256
VMEM_LIMIT_BYTES = 58 * 1024 * 1024

F32 = jnp.float32
BF16 = jnp.bfloat16


def _rms_scale(x):
    return lax.rsqrt(jnp.mean(x * x, axis=-1, keepdims=True) + EPS)


def _layer_row_spec(layer, width, n_grid):
    zeros = (0,) * 2
    if n_grid == 1:
        return pl.BlockSpec((None, 1, width), lambda i: (layer, *zeros))
    if n_grid == 2:
        return pl.BlockSpec((None, 1, width), lambda i, j: (layer, *zeros))
    return pl.BlockSpec((None, 1, width), lambda i, j, k: (layer, *zeros))


def _ffn_kernel(x_ref, gain_ref, wg_ref, wu_ref, wd_ref, o_ref, h_ref):
    @pl.when(pl.program_id(1) == 0)
    def _():
        x = x_ref[...]
        h_ref[...] = (x * _rms_scale(x) * gain_ref[...]).astype(BF16)
        o_ref[...] = x

    h = h_ref[...]
    g = jnp.dot(h, wg_ref[...].astype(BF16), preferred_element_type=F32)
    u = jnp.dot(h, wu_ref[...].astype(BF16), preferred_element_type=F32)
    a = (g * jax.nn.sigmoid(g)) * u * 0.5
    o_ref[...] += jnp.dot(a.astype(BF16), wd_ref[...].astype(BF16),
                          preferred_element_type=F32)


def _ffn(x, gain, wg, wu, wd, *, layer, tm, tf):
    m, d = x.shape
    f = wg.shape[2]
    return pl.pallas_call(
        _ffn_kernel,
        grid=(m // tm, f // tf),
        in_specs=[
            pl.BlockSpec((tm, d), lambda i, j: (i, 0)),
            _layer_row_spec(layer, d, 2),
            pl.BlockSpec((None, d, tf), lambda i, j: (layer, 0, j)),
            pl.BlockSpec((None, d, tf), lambda i, j: (layer, 0, j)),
            pl.BlockSpec((None, tf, d), lambda i, j: (layer, j, 0)),
        ],
        out_specs=pl.BlockSpec((tm, d), lambda i, j: (i, 0)),
        out_shape=jax.ShapeDtypeStruct((m, d), F32),
        scratch_shapes=[pltpu.VMEM((tm, d), BF16)],
        compiler_params=pltpu.CompilerParams(
            dimension_semantics=("arbitrary", "arbitrary"),
            vmem_limit_bytes=VMEM_LIMIT_BYTES,
        ),
        name="ffn",
    )(x, gain, wg, wu, wd)


def _mix_in_kernel(x_ref, gain_ref, w_ref, rnn_ref, att_ref, h_ref, wb_ref, *, n_rnn):
    i, j = pl.program_id(0), pl.program_id(1)

    @pl.when(i == 0)
    def _():
        wb_ref[j] = w_ref[...].astype(BF16)

    @pl.when(j == 0)
    def _():
        x = x_ref[...]
        h_ref[...] = (x * _rms_scale(x) * gain_ref[...]).astype(BF16)

    res = jnp.dot(h_ref[...], wb_ref[j], preferred_element_type=F32)

    @pl.when(j < n_rnn)
    def _():
        rnn_ref[...] = res

    @pl.when(j >= n_rnn)
    def _():
        att_ref[...] = res.astype(att_ref.dtype)


def _mix_in(x, gain, w, *, layer, d_rnn2, tm, tn):
    m, d = x.shape
    n = w.shape[2]
    nj = n // tn
    n_rnn = d_rnn2 // tn
    return pl.pallas_call(
        functools.partial(_mix_in_kernel, n_rnn=n_rnn),
        grid=(m // tm, nj),
        in_specs=[
            pl.BlockSpec((tm, d), lambda i, j: (i, 0)),
            _layer_row_spec(layer, d, 2),
            pl.BlockSpec((None, d, tn),
                         lambda i, j: (layer, 0, jnp.where(i == 0, j, nj - 1))),
        ],
        out_specs=[
            pl.BlockSpec((tm, tn), lambda i, j: (i, jnp.minimum(j, n_rnn - 1))),
            pl.BlockSpec((tm, tn), lambda i, j: (i, jnp.maximum(j - n_rnn, 0))),
        ],
        out_shape=[jax.ShapeDtypeStruct((m, d_rnn2), F32),
                   jax.ShapeDtypeStruct((m, n - d_rnn2), BF16)],
        scratch_shapes=[pltpu.VMEM((tm, d), BF16), pltpu.VMEM((nj, d, tn), BF16)],
        compiler_params=pltpu.CompilerParams(
            dimension_semantics=("arbitrary", "arbitrary"),
            vmem_limit_bytes=VMEM_LIMIT_BYTES,
        ),
        name="mix_in",
    )(x, gain, w)


def _rnn_kernel(x_ref, g_ref, cw_ref, cb_ref, wa_ref, ba_ref, wx_ref, bx_ref,
                lru_ref, gain_ref, o_ref, xs_ref, hprev_ref):
    ts, c = x_ref.shape

    @pl.when(pl.program_id(1) == 0)
    def _():
        xs_ref[0:SUBLANES, :] = jnp.zeros((SUBLANES, c), F32)
        hprev_ref[...] = jnp.zeros_like(hprev_ref)

    x = x_ref[...]
    xs_ref[SUBLANES:, :] = x
    xc = cb_ref[...] + x * cw_ref[CONV_W - 1:CONV_W, :]
    for k in range(CONV_W - 1):
        start = SUBLANES - (CONV_W - 1 - k)
        xc = xc + xs_ref[start:start + ts, :] * cw_ref[k:k + 1, :]
    xs_ref[0:SUBLANES, :] = x[ts - SUBLANES:, :]

    xcb = xc.astype(BF16)
    n_slabs = c // MXU_DIM

    def gate(w_ref, b_ref):
        parts = [
            jnp.dot(xcb[:, s * MXU_DIM:(s + 1) * MXU_DIM], w_ref[s],
                    preferred_element_type=F32)
            for s in range(n_slabs)
        ]
        return jax.nn.sigmoid(jnp.concatenate(parts, axis=1) + b_ref[...])

    r = gate(wa_ref, ba_ref)
    i = gate(wx_ref, bx_ref)

    neg_l = -lru_ref[...]
    softplus = jnp.maximum(neg_l, 0.0) + jnp.log1p(jnp.exp(-jnp.abs(neg_l)))
    log_a = (-LRU_C * r) * softplus
    a = jnp.exp(log_a)
    mult = jnp.sqrt(-jnp.tanh(log_a) * (1.0 + a * a))
    u = mult * (i * xc)

    sub = lax.broadcasted_iota(jnp.int32, (SUBLANES, c), 0)
    carry = hprev_ref[0:1, :]
    groups = []
    for g in range(ts // SUBLANES):
        rows = slice(g * SUBLANES, (g + 1) * SUBLANES)
        acc_a, acc_h = a[rows, :], u[rows, :]
        k = 1
        while k < SUBLANES:
            keep = sub >= k
            sh_a = pltpu.roll(acc_a, k, axis=0)
            sh_h = pltpu.roll(acc_h, k, axis=0)
            acc_h = jnp.where(keep, acc_a * sh_h + acc_h, acc_h)
            acc_a = jnp.where(keep, acc_a * sh_a, acc_a)
            k *= 2
        h_g = acc_h + acc_a * carry
        carry = h_g[SUBLANES - 1:SUBLANES, :]
        groups.append(h_g)
    h = jnp.concatenate(groups, axis=0)
    hprev_ref[0:1, :] = carry

    y = jax.nn.gelu(g_ref[...]) * h
    o_ref[...] = (y * _rms_scale(y) * gain_ref[...]).astype(o_ref.dtype)


def _rnn(p, conv_w, conv_b, wa_bd, ba, wx_bd, bx, lru, gain, *, layer, batch, seq, ts):
    c = conv_w.shape[2]
    nt = seq // ts
    row = lambda b, s: (b * nt + s, 0)
    gate_row = lambda b, s: (b * nt + s, 1)
    vec = _layer_row_spec(layer, c, 2)
    return pl.pallas_call(
        _rnn_kernel,
        grid=(batch, nt),
        in_specs=[
            pl.BlockSpec((ts, c), row),
            pl.BlockSpec((ts, c), gate_row),
            pl.BlockSpec((None, CONV_W, c), lambda b, s: (layer, 0, 0)),
            vec,
            pl.BlockSpec(wa_bd.shape, lambda b, s: (0, 0, 0)),
            vec,
            pl.BlockSpec(wx_bd.shape, lambda b, s: (0, 0, 0)),
            vec,
            vec,
            vec,
        ],
        out_specs=pl.BlockSpec((ts, c), row),
        out_shape=jax.ShapeDtypeStruct((batch * seq, c), BF16),
        scratch_shapes=[pltpu.VMEM((SUBLANES + ts, c), F32), pltpu.VMEM((SUBLANES, c), F32)],
        compiler_params=pltpu.CompilerParams(
            dimension_semantics=("arbitrary", "arbitrary"),
            vmem_limit_bytes=VMEM_LIMIT_BYTES,
        ),
        name="rnn",
    )(p, p, conv_w, conv_b, wa_bd, ba, wx_bd, bx, lru, gain)


def _qk_prep(t, gain, cos, sin_signed):
    lane = lax.broadcasted_iota(jnp.int32, t.shape, 1)
    lo = lane < ATT_DH
    t2 = t * t
    ss_lo = jnp.sum(jnp.where(lo, t2, 0.0), axis=-1, keepdims=True)
    ss_hi = jnp.sum(jnp.where(lo, 0.0, t2), axis=-1, keepdims=True)
    ms = jnp.where(lo, ss_lo, ss_hi) * (1.0 / ATT_DH)
    y = t * lax.rsqrt(ms + EPS) * gain
    half = ATT_DH // 2
    ahead = pltpu.roll(y, LANES - half, axis=1)
    behind = pltpu.roll(y, half, axis=1)
    rot = jnp.where((lane & half) == 0, ahead, behind)
    return y * cos + rot * sin_signed


def _attn_kernel(q_ref, k_ref, v_ref, cos_ref, sin_ref, qn_ref, kn_ref, lam_ref,
                 subln_ref, o_ref, kh_ref, vt_ref, *, tq, lam_init):
    seq = q_ref.shape[0]
    kh_ref[...] = _qk_prep(k_ref[...].astype(F32), kn_ref[...], cos_ref[...],
                           sin_ref[...]).astype(BF16)
    vt_ref[0:ATT_DV, :] = v_ref[...].astype(F32).T.astype(BF16)
    vt_ref[ATT_DV:, :] = jnp.ones((vt_ref.shape[0] - ATT_DV, seq), BF16)

    lp = lam_ref[...]
    lam = (jnp.exp(jnp.sum(lp[0:1] * lp[1:2], axis=-1, keepdims=True))
           - jnp.exp(jnp.sum(lp[2:3] * lp[3:4], axis=-1, keepdims=True))
           + lam_init)

    lane = lax.broadcasted_iota(jnp.int32, (tq, ATT_DV), 1)
    lo = lane < ATT_DH
    causal_t = (lax.broadcasted_iota(jnp.int32, (tq, 2 * tq), 0)
                <= (lax.broadcasted_iota(jnp.int32, (tq, 2 * tq), 1) & (tq - 1)))
    nt_dims = (((1,), (1,)), ((), ()))

    def score_phase(blk, out):
        rows = slice(blk * tq, (blk + 1) * tq)
        qh = _qk_prep(q_ref[rows, :].astype(F32), qn_ref[...], cos_ref[rows, :],
                      sin_ref[rows, :])
        qh = qh * (math.log2(math.e) / math.sqrt(ATT_DH))
        q_both = jnp.concatenate([jnp.where(lo, qh, 0.0), jnp.where(lo, 0.0, qh)],
                                 axis=0).astype(BF16)
        chunks, m = [], None
        for c in range(blk + 1):
            st = lax.dot_general(kh_ref[c * tq:(c + 1) * tq, :], q_both, nt_dims,
                                 preferred_element_type=F32)
            if c == blk:
                st = jnp.where(causal_t, st, NEG_INF)
            cm = jnp.max(st, axis=0, keepdims=True)
            m = cm if m is None else jnp.maximum(m, cm)
            chunks.append(st)
            yield
        out.append((chunks, m))

    def value_phase(blk, chunks, m):
        acc = None
        for c, st in enumerate(chunks):
            part = jnp.dot(vt_ref[:, c * tq:(c + 1) * tq], jnp.exp2(st - m).astype(BF16),
                           preferred_element_type=F32)
            acc = part if acc is None else acc + part
            yield
        l = acc[ATT_DV:ATT_DV + 1, :]
        a1, a2 = acc[0:ATT_DV, 0:tq], acc[0:ATT_DV, tq:]
        l1, l2 = l[:, 0:tq], l[:, tq:]
        ot = a1 * (1.0 / l1) - a2 * (lam / l2)
        ot = ot * lax.rsqrt(jnp.mean(ot * ot, axis=0, keepdims=True) + EPS)
        o = ot.T * (subln_ref[...] * (1.0 - lam_init))
        o_ref[blk * tq:(blk + 1) * tq, :] = o.astype(o_ref.dtype)

    n_blk = seq // tq
    scored = []
    for _ in score_phase(0, scored):
        pass
    for blk in range(n_blk):
        phases = [value_phase(blk, *scored[blk])]
        if blk + 1 < n_blk:
            phases.append(score_phase(blk + 1, scored))
        while phases:
            for ph in list(phases):
                if next(ph, StopIteration) is StopIteration:
                    phases.remove(ph)


def _attn(p, cos, sin_signed, qn, kn, lam_params, subln, *, layer, batch, seq, tq,
          lam_init, q_col, k_col, v_col):
    d = ATT_DV
    const2 = lambda b, h: (0, 0)
    vec = _layer_row_spec(layer, d, 2)
    return pl.pallas_call(
        functools.partial(_attn_kernel, tq=tq, lam_init=lam_init),
        grid=(batch, ATT_HEADS),
        in_specs=[
            pl.BlockSpec((seq, d), lambda b, h: (b, q_col + h)),
            pl.BlockSpec((seq, d), lambda b, h: (b, k_col + h)),
            pl.BlockSpec((seq, d), lambda b, h: (b, v_col + h)),
            pl.BlockSpec((seq, d), const2),
            pl.BlockSpec((seq, d), const2),
            vec,
            vec,
            pl.BlockSpec((None, 4, ATT_DH), lambda b, h: (layer, 0, 0)),
            vec,
        ],
        out_specs=pl.BlockSpec((seq, d), lambda b, h: (b, h)),
        out_shape=jax.ShapeDtypeStruct((batch * seq, ATT_HEADS * d), BF16),
        scratch_shapes=[pltpu.VMEM((seq, d), BF16), pltpu.VMEM((d + BF16_SUBLANES, seq), BF16)],
        compiler_params=pltpu.CompilerParams(
            dimension_semantics=("arbitrary", "arbitrary"),
            vmem_limit_bytes=VMEM_LIMIT_BYTES,
        ),
        name="diff_attn",
    )(p, p, p, cos, sin_signed, qn, kn, lam_params, subln)


def _out_proj_kernel(x_ref, rnn_ref, att_ref, w_ref, o_ref, wb_ref):
    @pl.when(pl.program_id(0) == 0)
    def _():
        wb_ref[...] = w_ref[...].astype(BF16)

    c = rnn_ref.shape[1]
    mixed = jnp.dot(rnn_ref[...], wb_ref[:c, :], preferred_element_type=F32)
    mixed = mixed + jnp.dot(att_ref[...], wb_ref[c:, :], preferred_element_type=F32)
    o_ref[...] = x_ref[...] + mixed


def _out_proj(x, rnn, att, w, *, layer, tm):
    m, d = x.shape
    c = rnn.shape[1]
    k = w.shape[1]
    return pl.pallas_call(
        _out_proj_kernel,
        grid=(m // tm,),
        in_specs=[
            pl.BlockSpec((tm, d), lambda i: (i, 0)),
            pl.BlockSpec((tm, c), lambda i: (i, 0)),
            pl.BlockSpec((tm, att.shape[1]), lambda i: (i, 0)),
            pl.BlockSpec((None, k, d), lambda i: (layer, 0, 0),
                         pipeline_mode=pl.Buffered(1)),
        ],
        out_specs=pl.BlockSpec((tm, d), lambda i: (i, 0)),
        out_shape=jax.ShapeDtypeStruct((m, d), F32),
        scratch_shapes=[pltpu.VMEM((k, d), BF16)],
        compiler_params=pltpu.CompilerParams(
            dimension_semantics=("arbitrary",),
            vmem_limit_bytes=VMEM_LIMIT_BYTES,
        ),
        name="out_proj",
    )(x, rnn, att, w)


def _block_diag_slabs(w):
    g, n, _ = w.shape
    per = MXU_DIM // n
    w5 = w.reshape(g // per, per, n, 1, n)
    eye = jnp.eye(per, dtype=bool)[None, :, None, :, None]
    return jnp.where(eye, w5, 0.0).reshape(g // per, MXU_DIM, MXU_DIM)


def _rope_tables(seq):
    inv_freq = ROPE_THETA ** (-jnp.arange(0, ATT_DH, 2, dtype=F32) / ATT_DH)
    ang = jnp.arange(seq, dtype=F32)[:, None] * inv_freq[None, :]
    ang = jnp.concatenate([ang, ang], axis=-1)
    cos, sin = jnp.cos(ang), jnp.sin(ang)
    half = ATT_DH // 2
    sin_signed = jnp.concatenate([-sin[:, :half], sin[:, half:]], axis=-1)
    return jnp.tile(cos, (1, 2)), jnp.tile(sin_signed, (1, 2))


def kernel(x, ffn1_norm, ffn1_w_gate, ffn1_w_up, ffn1_w_down, mix_norm, w_in, conv_w, conv_b, gate_a_w, gate_a_b, gate_x_w, gate_x_b, lru_param, rnn_out_norm, q_norm, k_norm, lam_q1, lam_k1, lam_q2, lam_k2, subln, w_out, ffn2_norm, ffn2_w_gate, ffn2_w_up, ffn2_w_down):
    bsz, seq, d = x.shape
    depth = w_in.shape[0]
    d_rnn = conv_w.shape[-1]
    d_q = ATT_HEADS * 2 * ATT_DH
    assert w_in.shape[-1] == 2 * d_rnn + 2 * d_q + ATT_HEADS * ATT_DV
    assert d_rnn % LANES == 0 and d_q % LANES == 0
    q_col = 0
    k_col = q_col + d_q // ATT_DV
    v_col = k_col + d_q // ATT_DV

    cos, sin_signed = _rope_tables(seq)
    xf = x.reshape(bsz * seq, d)
    rows = lambda v: v.reshape(depth, 1, -1)
    ffn1_norm, ffn2_norm, mix_norm = rows(ffn1_norm), rows(ffn2_norm), rows(mix_norm)
    conv_b, lru_param, rnn_out_norm = rows(conv_b), rows(lru_param), rows(rnn_out_norm)
    gate_a_b, gate_x_b, subln = rows(gate_a_b), rows(gate_x_b), rows(subln)
    q_gain = rows(jnp.tile(q_norm, (1, 2)))
    k_gain = rows(jnp.tile(k_norm, (1, 2)))
    lam_params = jnp.stack([lam_q1, lam_k1, lam_q2, lam_k2], axis=1)

    for l in range(depth):
        xf = _ffn(xf, ffn1_norm, ffn1_w_gate, ffn1_w_up, ffn1_w_down,
                  layer=l, tm=1024, tf=256)

        p_rnn, p_att = _mix_in(xf, mix_norm, w_in, layer=l, d_rnn2=2 * d_rnn,
                               tm=1024, tn=512)

        out_rnn = _rnn(
            p_rnn, conv_w, conv_b,
            _block_diag_slabs(gate_a_w[l]).astype(BF16), gate_a_b,
            _block_diag_slabs(gate_x_w[l]).astype(BF16), gate_x_b,
            lru_param, rnn_out_norm, layer=l, batch=bsz, seq=seq, ts=512)

        lam_init = 0.8 - 0.6 * math.exp(-0.3 * l)
        out_att = _attn(
            p_att, cos, sin_signed, q_gain, k_gain, lam_params, subln,
            layer=l, batch=bsz, seq=seq, tq=256, lam_init=lam_init,
            q_col=q_col, k_col=k_col, v_col=v_col)

        xf = _out_proj(xf, out_rnn, out_att, w_out, layer=l, tm=512)

        xf = _ffn(xf, ffn2_norm, ffn2_w_gate, ffn2_w_up, ffn2_w_down,
                  layer=l, tm=1024, tf=256)
    return xf.reshape(bsz, seq, d)
```

```python
import functools
import math

import jax
import jax.numpy as jnp
from jax import lax
from jax.experimental import pallas as pl
from jax.experimental.pallas import tpu as pltpu

RNN_BLOCKS = 16
CONV_W = 4
LRU_C = 8.0
ATT_HEADS = 8
ATT_DH = 64
ATT_DV = 2 * ATT_DH
ROPE_THETA = 10000.0
EPS = 1e-6
NEG_INF = -1e30

LANES = 128
SUBLANES = 8
BF16_SUBLANES = 16
MXU_DIM = 256
VMEM_CAPACITY_BYTES = 64 * 1024 * 1024
VMEM_LIMIT_BYTES = 58 * 1024 * 1024
VMEM_SPILL_HEADROOM_BYTES = 8 * 1024 * 1024

F32 = jnp.float32
BF16 = jnp.bfloat16


def _rms_scale(x):
    return lax.rsqrt(jnp.mean(x * x, axis=-1, keepdims=True) + EPS)


def _layer_row_spec(layer, width, n_grid):
    zeros = (0,) * 2
    if n_grid == 1:
        return pl.BlockSpec((None, 1, width), lambda i: (layer, *zeros))
    if n_grid == 2:
        return pl.BlockSpec((None, 1, width), lambda i, j: (layer, *zeros))
    return pl.BlockSpec((None, 1, width), lambda i, j, k: (layer, *zeros))


def _ffn_kernel(x_hbm, gain_ref, wg_ref, wu_ref, wd_ref, o_ref, xs_ref, h_ref, sem):
    i, j = pl.program_id(0), pl.program_id(1)
    tm = xs_ref.shape[0]

    def x_tile_copy(tile):
        start = pl.multiple_of(tile * tm, tm)
        return pltpu.make_async_copy(x_hbm.at[pl.ds(start, tm), :], xs_ref, sem)

    @pl.when((i == 0) & (j == 0))
    def _():
        x_tile_copy(0).start()

    @pl.when(j == 0)
    def _():
        x_tile_copy(i).wait()
        x = xs_ref[...]
        h_ref[...] = (x * _rms_scale(x) * gain_ref[...]).astype(BF16)
        o_ref[...] = x

    @pl.when((j == 1) & (i + 1 < pl.num_programs(0)))
    def _():
        x_tile_copy(i + 1).start()

    h = h_ref[...]
    g = jnp.dot(h, wg_ref[...].astype(BF16), preferred_element_type=F32)
    u = jnp.dot(h, wu_ref[...].astype(BF16), preferred_element_type=F32)
    a = (g * jax.nn.sigmoid(g)) * u * 0.5
    o_ref[...] += jnp.dot(a.astype(BF16), wd_ref[...].astype(BF16),
                          preferred_element_type=F32)


def _ffn(x, gain, wg, wu, wd, *, layer, tm, tf):
    m, d = x.shape
    f = wg.shape[2]
    window_bytes = 2 * tm * d * 4 + 2 * 3 * d * tf * 4 + tm * d * 4 + tm * d * 2
    vmem_limit = window_bytes + VMEM_SPILL_HEADROOM_BYTES
    assert vmem_limit <= VMEM_CAPACITY_BYTES
    return pl.pallas_call(
        _ffn_kernel,
        grid=(m // tm, f // tf),
        in_specs=[
            pl.BlockSpec(memory_space=pl.ANY),
            _layer_row_spec(layer, d, 2),
            pl.BlockSpec((None, d, tf), lambda i, j: (layer, 0, j)),
            pl.BlockSpec((None, d, tf), lambda i, j: (layer, 0, j)),
            pl.BlockSpec((None, tf, d), lambda i, j: (layer, j, 0)),
        ],
        out_specs=pl.BlockSpec((tm, d), lambda i, j: (i, 0)),
        out_shape=jax.ShapeDtypeStruct((m, d), F32),
        scratch_shapes=[pltpu.VMEM((tm, d), F32), pltpu.VMEM((tm, d), BF16),
                        pltpu.SemaphoreType.DMA(())],
        compiler_params=pltpu.CompilerParams(
            dimension_semantics=("arbitrary", "arbitrary"),
            vmem_limit_bytes=vmem_limit,
        ),
        name="ffn",
    )(x, gain, wg, wu, wd)


def _mix_in_kernel(x_ref, gain_ref, w_ref, o_ref, h_ref, wb_ref, *, row_chunks):
    i, j = pl.program_id(0), pl.program_id(1)

    @pl.when(i == 0)
    def _():
        wb_ref[j] = w_ref[...].astype(BF16)

    @pl.when(j == 0)
    def _():
        x = x_ref[...]
        h_ref[...] = (x * _rms_scale(x) * gain_ref[...]).astype(BF16)

    w = wb_ref[j]
    rc = h_ref.shape[0] // row_chunks
    for r in range(row_chunks):
        rows = slice(r * rc, (r + 1) * rc)
        o_ref[rows, :] = jnp.dot(h_ref[rows, :], w, preferred_element_type=F32)


def _mix_in(x, gain, w, *, layer, tm, tn, row_chunks):
    m, d = x.shape
    n = w.shape[2]
    nj = n // tn
    return pl.pallas_call(
        functools.partial(_mix_in_kernel, row_chunks=row_chunks),
        grid=(m // tm, nj),
        in_specs=[
            pl.BlockSpec((tm, d), lambda i, j: (i, 0)),
            _layer_row_spec(layer, d, 2),
            pl.BlockSpec((None, d, tn),
                         lambda i, j: (layer, 0, jnp.where(i == 0, j, nj - 1))),
        ],
        out_specs=pl.BlockSpec((tm, tn), lambda i, j: (i, j)),
        out_shape=jax.ShapeDtypeStruct((m, n), F32),
        scratch_shapes=[pltpu.VMEM((tm, d), BF16), pltpu.VMEM((nj, d, tn), BF16)],
        compiler_params=pltpu.CompilerParams(
            dimension_semantics=("arbitrary", "arbitrary"),
            vmem_limit_bytes=VMEM_LIMIT_BYTES,
        ),
        name="mix_in",
    )(x, gain, w)


def _rnn_kernel(x_ref, g_ref, cw_ref, cb_ref, wa_ref, ba_ref, wx_ref, bx_ref,
                lru_ref, gain_ref, o_ref, xs_ref, hprev_ref):
    ts, c = x_ref.shape

    @pl.when(pl.program_id(1) == 0)
    def _():
        xs_ref[0:SUBLANES, :] = jnp.zeros((SUBLANES, c), F32)
        hprev_ref[...] = jnp.zeros_like(hprev_ref)

    x = x_ref[...]
    xs_ref[SUBLANES:, :] = x
    xc = cb_ref[...] + x * cw_ref[CONV_W - 1:CONV_W, :]
    for k in range(CONV_W - 1):
        start = SUBLANES - (CONV_W - 1 - k)
        xc = xc + xs_ref[start:start + ts, :] * cw_ref[k:k + 1, :]
    xs_ref[0:SUBLANES, :] = x[ts - SUBLANES:, :]

    xcb = xc.astype(BF16)
    n_slabs = c // MXU_DIM

    def gate(w_ref, b_ref):
        parts = [
            jnp.dot(xcb[:, s * MXU_DIM:(s + 1) * MXU_DIM], w_ref[s],
                    preferred_element_type=F32)
            for s in range(n_slabs)
        ]
        return jax.nn.sigmoid(jnp.concatenate(parts, axis=1) + b_ref[...])

    r = gate(wa_ref, ba_ref)
    i = gate(wx_ref, bx_ref)

    neg_l = -lru_ref[...]
    softplus = jnp.maximum(neg_l, 0.0) + jnp.log1p(jnp.exp(-jnp.abs(neg_l)))
    log_a = (-LRU_C * r) * softplus
    a = jnp.exp(log_a)
    mult = jnp.sqrt(-jnp.tanh(log_a) * (1.0 + a * a))
    u = mult * (i * xc)

    sub = lax.broadcasted_iota(jnp.int32, (SUBLANES, c), 0)
    carry = hprev_ref[0:1, :]
    groups = []
    for g in range(ts // SUBLANES):
        rows = slice(g * SUBLANES, (g + 1) * SUBLANES)
        acc_a, acc_h = a[rows, :], u[rows, :]
        k = 1
        while k < SUBLANES:
            keep = sub >= k
            sh_a = pltpu.roll(acc_a, k, axis=0)
            sh_h = pltpu.roll(acc_h, k, axis=0)
            acc_h = jnp.where(keep, acc_a * sh_h + acc_h, acc_h)
            acc_a = jnp.where(keep, acc_a * sh_a, acc_a)
            k *= 2
        h_g = acc_h + acc_a * carry
        carry = h_g[SUBLANES - 1:SUBLANES, :]
        groups.append(h_g)
    h = jnp.concatenate(groups, axis=0)
    hprev_ref[0:1, :] = carry

    y = jax.nn.gelu(g_ref[...]) * h
    o_ref[...] = (y * _rms_scale(y) * gain_ref[...]).astype(o_ref.dtype)


def _rnn(p, conv_w, conv_b, wa_bd, ba, wx_bd, bx, lru, gain, *, layer, batch, seq, ts):
    c = conv_w.shape[2]
    nt = seq // ts
    row = lambda b, s: (b * nt + s, 0)
    gate_row = lambda b, s: (b * nt + s, 1)
    vec = _layer_row_spec(layer, c, 2)
    return pl.pallas_call(
        _rnn_kernel,
        grid=(batch, nt),
        in_specs=[
            pl.BlockSpec((ts, c), row),
            pl.BlockSpec((ts, c), gate_row),
            pl.BlockSpec((None, CONV_W, c), lambda b, s: (layer, 0, 0)),
            vec,
            pl.BlockSpec(wa_bd.shape, lambda b, s: (0, 0, 0)),
            vec,
            pl.BlockSpec(wx_bd.shape, lambda b, s: (0, 0, 0)),
            vec,
            vec,
            vec,
        ],
        out_specs=pl.BlockSpec((ts, c), row),
        out_shape=jax.ShapeDtypeStruct((batch * seq, c), BF16),
        scratch_shapes=[pltpu.VMEM((SUBLANES + ts, c), F32), pltpu.VMEM((SUBLANES, c), F32)],
        compiler_params=pltpu.CompilerParams(
            dimension_semantics=("arbitrary", "arbitrary"),
            vmem_limit_bytes=VMEM_LIMIT_BYTES,
        ),
        name="rnn",
    )(p, p, conv_w, conv_b, wa_bd, ba, wx_bd, bx, lru, gain)


def _qk_prep(t, gain, cos, sin_signed):
    lane = lax.broadcasted_iota(jnp.int32, t.shape, 1)
    lo = lane < ATT_DH
    t2 = t * t
    ss_lo = jnp.sum(jnp.where(lo, t2, 0.0), axis=-1, keepdims=True)
    ss_hi = jnp.sum(jnp.where(lo, 0.0, t2), axis=-1, keepdims=True)
    ms = jnp.where(lo, ss_lo, ss_hi) * (1.0 / ATT_DH)
    y = t * lax.rsqrt(ms + EPS) * gain
    half = ATT_DH // 2
    ahead = pltpu.roll(y, LANES - half, axis=1)
    behind = pltpu.roll(y, half, axis=1)
    rot = jnp.where((lane & half) == 0, ahead, behind)
    return y * cos + rot * sin_signed


def _attn_kernel(q_ref, k_ref, v_ref, cos_ref, sin_ref, qn_ref, kn_ref, lam_ref,
                 subln_ref, o_ref, kh_ref, vt_ref, *, tq, lam_init):
    seq = q_ref.shape[0]
    kh_ref[...] = _qk_prep(k_ref[...].astype(F32), kn_ref[...], cos_ref[...],
                           sin_ref[...]).astype(BF16)
    vt_ref[0:ATT_DV, :] = v_ref[...].astype(F32).T.astype(BF16)
    vt_ref[ATT_DV:, :] = jnp.ones((vt_ref.shape[0] - ATT_DV, seq), BF16)

    lp = lam_ref[...]
    lam = (jnp.exp(jnp.sum(lp[0:1] * lp[1:2], axis=-1, keepdims=True))
           - jnp.exp(jnp.sum(lp[2:3] * lp[3:4], axis=-1, keepdims=True))
           + lam_init)

    lane = lax.broadcasted_iota(jnp.int32, (tq, ATT_DV), 1)
    lo = lane < ATT_DH
    causal_t = (lax.broadcasted_iota(jnp.int32, (tq, 2 * tq), 0)
                <= (lax.broadcasted_iota(jnp.int32, (tq, 2 * tq), 1) & (tq - 1)))
    nt_dims = (((1,), (1,)), ((), ()))

    def score_phase(blk, out):
        rows = slice(blk * tq, (blk + 1) * tq)
        qh = _qk_prep(q_ref[rows, :].astype(F32), qn_ref[...], cos_ref[rows, :],
                      sin_ref[rows, :])
        qh = qh * (math.log2(math.e) / math.sqrt(ATT_DH))
        q_both = jnp.concatenate([jnp.where(lo, qh, 0.0), jnp.where(lo, 0.0, qh)],
                                 axis=0).astype(BF16)
        chunks, m = [], None
        for c in range(blk + 1):
            st = lax.dot_general(kh_ref[c * tq:(c + 1) * tq, :], q_both, nt_dims,
                                 preferred_element_type=F32)
            if c == blk:
                st = jnp.where(causal_t, st, NEG_INF)
            cm = jnp.max(st, axis=0, keepdims=True)
            m = cm if m is None else jnp.maximum(m, cm)
            chunks.append(st)
            yield
        out.append((chunks, m))

    def value_phase(blk, chunks, m):
        acc = None
        for c, st in enumerate(chunks):
            part = jnp.dot(vt_ref[:, c * tq:(c + 1) * tq], jnp.exp2(st - m).astype(BF16),
                           preferred_element_type=F32)
            acc = part if acc is None else acc + part
            yield
        l = acc[ATT_DV:ATT_DV + 1, :]
        a1, a2 = acc[0:ATT_DV, 0:tq], acc[0:ATT_DV, tq:]
        l1, l2 = l[:, 0:tq], l[:, tq:]
        ot = a1 * (1.0 / l1) - a2 * (lam / l2)
        ot = ot * lax.rsqrt(jnp.mean(ot * ot, axis=0, keepdims=True) + EPS)
        o = ot.T * (subln_ref[...] * (1.0 - lam_init))
        o_ref[blk * tq:(blk + 1) * tq, :] = o.astype(o_ref.dtype)

    n_blk = seq // tq
    scored = []
    for _ in score_phase(0, scored):
        pass
    for blk in range(n_blk):
        phases = [value_phase(blk, *scored[blk])]
        if blk + 1 < n_blk:
            phases.append(score_phase(blk + 1, scored))
        while phases:
            for ph in list(phases):
                if next(ph, StopIteration) is StopIteration:
                    phases.remove(ph)


def _attn(p, cos, sin_signed, qn, kn, lam_params, subln, *, layer, batch, seq, tq,
          lam_init, q_col, k_col, v_col):
    d = ATT_DV
    const2 = lambda b, h: (0, 0)
    vec = _layer_row_spec(layer, d, 2)
    return pl.pallas_call(
        functools.partial(_attn_kernel, tq=tq, lam_init=lam_init),
        grid=(batch, ATT_HEADS),
        in_specs=[
            pl.BlockSpec((seq, d), lambda b, h: (b, q_col + h)),
            pl.BlockSpec((seq, d), lambda b, h: (b, k_col + h)),
            pl.BlockSpec((seq, d), lambda b, h: (b, v_col + h)),
            pl.BlockSpec((seq, d), const2),
            pl.BlockSpec((seq, d), const2),
            vec,
            vec,
            pl.BlockSpec((None, 4, ATT_DH), lambda b, h: (layer, 0, 0)),
            vec,
        ],
        out_specs=pl.BlockSpec((seq, d), lambda b, h: (b, h)),
        out_shape=jax.ShapeDtypeStruct((batch * seq, ATT_HEADS * d), BF16),
        scratch_shapes=[pltpu.VMEM((seq, d), BF16), pltpu.VMEM((d + BF16_SUBLANES, seq), BF16)],
        compiler_params=pltpu.CompilerParams(
            dimension_semantics=("arbitrary", "arbitrary"),
            vmem_limit_bytes=VMEM_LIMIT_BYTES,
        ),
        name="diff_attn",
    )(p, p, p, cos, sin_signed, qn, kn, lam_params, subln)


def _out_proj_kernel(x_ref, rnn_ref, att_ref, w_ref, o_ref, wb_ref):
    @pl.when(pl.program_id(0) == 0)
    def _():
        wb_ref[...] = w_ref[...].astype(BF16)

    c = rnn_ref.shape[1]
    mixed = jnp.dot(rnn_ref[...], wb_ref[:c, :], preferred_element_type=F32)
    mixed = mixed + jnp.dot(att_ref[...], wb_ref[c:, :], preferred_element_type=F32)
    o_ref[...] = x_ref[...] + mixed


def _out_proj(x, rnn, att, w, *, layer, tm):
    m, d = x.shape
    c = rnn.shape[1]
    k = w.shape[1]
    return pl.pallas_call(
        _out_proj_kernel,
        grid=(m // tm,),
        in_specs=[
            pl.BlockSpec((tm, d), lambda i: (i, 0)),
            pl.BlockSpec((tm, c), lambda i: (i, 0)),
            pl.BlockSpec((tm, att.shape[1]), lambda i: (i, 0)),
            pl.BlockSpec((None, k, d), lambda i: (layer, 0, 0),
                         pipeline_mode=pl.Buffered(1)),
        ],
        out_specs=pl.BlockSpec((tm, d), lambda i: (i, 0)),
        out_shape=jax.ShapeDtypeStruct((m, d), F32),
        scratch_shapes=[pltpu.VMEM((k, d), BF16)],
        compiler_params=pltpu.CompilerParams(
            dimension_semantics=("arbitrary",),
            vmem_limit_bytes=VMEM_LIMIT_BYTES,
        ),
        name="out_proj",
    )(x, rnn, att, w)


def _block_diag_slabs(w):
    g, n, _ = w.shape
    per = MXU_DIM // n
    w5 = w.reshape(g // per, per, n, 1, n)
    eye = jnp.eye(per, dtype=bool)[None, :, None, :, None]
    return jnp.where(eye, w5, 0.0).reshape(g // per, MXU_DIM, MXU_DIM)


def _rope_tables(seq):
    inv_freq = ROPE_THETA ** (-jnp.arange(0, ATT_DH, 2, dtype=F32) / ATT_DH)
    ang = jnp.arange(seq, dtype=F32)[:, None] * inv_freq[None, :]
    ang = jnp.concatenate([ang, ang], axis=-1)
    cos, sin = jnp.cos(ang), jnp.sin(ang)
    half = ATT_DH // 2
    sin_signed = jnp.concatenate([-sin[:, :half], sin[:, half:]], axis=-1)
    return jnp.tile(cos, (1, 2)), jnp.tile(sin_signed, (1, 2))


def kernel(x, ffn1_norm, ffn1_w_gate, ffn1_w_up, ffn1_w_down, mix_norm, w_in, conv_w, conv_b, gate_a_w, gate_a_b, gate_x_w, gate_x_b, lru_param, rnn_out_norm, q_norm, k_norm, lam_q1, lam_k1, lam_q2, lam_k2, subln, w_out, ffn2_norm, ffn2_w_gate, ffn2_w_up, ffn2_w_down):
    bsz, seq, d = x.shape
    depth = w_in.shape[0]
    d_rnn = conv_w.shape[-1]
    d_q = ATT_HEADS * 2 * ATT_DH
    assert w_in.shape[-1] == 2 * d_rnn + 2 * d_q + ATT_HEADS * ATT_DV
    assert d_rnn % LANES == 0 and d_q % LANES == 0
    q_col = 2 * d_rnn // ATT_DV
    k_col = q_col + d_q // ATT_DV
    v_col = k_col + d_q // ATT_DV

    cos, sin_signed = _rope_tables(seq)
    xf = x.reshape(bsz * seq, d)
    rows = lambda v: v.reshape(depth, 1, -1)
    ffn1_norm, ffn2_norm, mix_norm = rows(ffn1_norm), rows(ffn2_norm), rows(mix_norm)
    conv_b, lru_param, rnn_out_norm = rows(conv_b), rows(lru_param), rows(rnn_out_norm)
    gate_a_b, gate_x_b, subln = rows(gate_a_b), rows(gate_x_b), rows(subln)
    q_gain = rows(jnp.tile(q_norm, (1, 2)))
    k_gain = rows(jnp.tile(k_norm, (1, 2)))
    lam_params = jnp.stack([lam_q1, lam_k1, lam_q2, lam_k2], axis=1)

    for l in range(depth):
        xf = _ffn(xf, ffn1_norm, ffn1_w_gate, ffn1_w_up, ffn1_w_down,
                  layer=l, tm=1024, tf=512)

        p = _mix_in(xf, mix_norm, w_in, layer=l, tm=1024, tn=512, row_chunks=4)

        out_rnn = _rnn(
            p, conv_w, conv_b,
            _block_diag_slabs(gate_a_w[l]).astype(BF16), gate_a_b,
            _block_diag_slabs(gate_x_w[l]).astype(BF16), gate_x_b,
            lru_param, rnn_out_norm, layer=l, batch=bsz, seq=seq, ts=512)

        lam_init = 0.8 - 0.6 * math.exp(-0.3 * l)
        out_att = _attn(
            p, cos, sin_signed, q_gain, k_gain, lam_params, subln,
            layer=l, batch=bsz, seq=seq, tq=256, lam_init=lam_init,
            q_col=q_col, k_col=k_col, v_col=v_col)

        xf = _out_proj(xf, out_rnn, out_att, w_out, layer=l, tm=512)

        xf = _ffn(xf, ffn2_norm, ffn2_w_gate, ffn2_w_up, ffn2_w_down,
                  layer=l, tm=1024, tf=512)
    return xf.reshape(bsz, seq, d)
```

```python
import functools
import math

import jax
import jax.numpy as jnp
from jax import lax
from jax.experimental import pallas as pl
from jax.experimental.pallas import tpu as pltpu

RNN_BLOCKS = 16
CONV_W = 4
LRU_C = 8.0
ATT_HEADS = 8
ATT_DH = 64
ATT_DV = 2 * ATT_DH
ROPE_THETA = 10000.0
EPS = 1e-6
NEG_INF = -1e30

LANES = 128
SUBLANES = 8
BF16_SUBLANES = 16
MXU_DIM = 256
VMEM_CAPACITY_BYTES = 64 * 1024 * 1024
VMEM_LIMIT_BYTES = 58 * 1024 * 1024
VMEM_SPILL_HEADROOM_BYTES = 8 * 1024 * 1024

F32 = jnp.float32
BF16 = jnp.bfloat16


def _rms_scale(x):
    return lax.rsqrt(jnp.mean(x * x, axis=-1, keepdims=True) + EPS)


def _layer_row_spec(layer, width, n_grid):
    zeros = (0,) * 2
    if n_grid == 1:
        return pl.BlockSpec((None, 1, width), lambda i: (layer, *zeros))
    if n_grid == 2:
        return pl.BlockSpec((None, 1, width), lambda i, j: (layer, *zeros))
    return pl.BlockSpec((None, 1, width), lambda i, j, k: (layer, *zeros))


def _on_staged_row_tile(x_hbm, xs_ref, sem, on_tile):
    i, j = pl.program_id(0), pl.program_id(1)
    tm = xs_ref.shape[0]

    def x_tile_copy(tile):
        start = pl.multiple_of(tile * tm, tm)
        return pltpu.make_async_copy(x_hbm.at[pl.ds(start, tm), :], xs_ref, sem)

    @pl.when((i == 0) & (j == 0))
    def _():
        x_tile_copy(0).start()

    @pl.when(j == 0)
    def _():
        x_tile_copy(i).wait()
        on_tile(xs_ref[...])

    @pl.when((j == 1) & (i + 1 < pl.num_programs(0)))
    def _():
        x_tile_copy(i + 1).start()


def _ffn_kernel(x_hbm, gain_ref, wg_ref, wu_ref, wd_ref, o_ref, xs_ref, h_ref, sem):
    def start_row_tile(x):
        h_ref[...] = (x * _rms_scale(x) * gain_ref[...]).astype(BF16)
        o_ref[...] = x

    _on_staged_row_tile(x_hbm, xs_ref, sem, start_row_tile)

    h = h_ref[...]
    g = jnp.dot(h, wg_ref[...].astype(BF16), preferred_element_type=F32)
    u = jnp.dot(h, wu_ref[...].astype(BF16), preferred_element_type=F32)
    a = (g * jax.nn.sigmoid(g)) * u * 0.5
    o_ref[...] += jnp.dot(a.astype(BF16), wd_ref[...].astype(BF16),
                          preferred_element_type=F32)


def _ffn(x, gain, wg, wu, wd, *, layer, tm, tf):
    m, d = x.shape
    f = wg.shape[2]
    assert f // tf >= 2
    window_bytes = 2 * tm * d * 4 + 2 * 3 * d * tf * 4 + tm * d * 4 + tm * d * 2
    vmem_limit = window_bytes + VMEM_SPILL_HEADROOM_BYTES
    assert vmem_limit <= VMEM_CAPACITY_BYTES
    return pl.pallas_call(
        _ffn_kernel,
        grid=(m // tm, f // tf),
        in_specs=[
            pl.BlockSpec(memory_space=pl.ANY),
            _layer_row_spec(layer, d, 2),
            pl.BlockSpec((None, d, tf), lambda i, j: (layer, 0, j)),
            pl.BlockSpec((None, d, tf), lambda i, j: (layer, 0, j)),
            pl.BlockSpec((None, tf, d), lambda i, j: (layer, j, 0)),
        ],
        out_specs=pl.BlockSpec((tm, d), lambda i, j: (i, 0)),
        out_shape=jax.ShapeDtypeStruct((m, d), F32),
        scratch_shapes=[pltpu.VMEM((tm, d), F32), pltpu.VMEM((tm, d), BF16),
                        pltpu.SemaphoreType.DMA(())],
        compiler_params=pltpu.CompilerParams(
            dimension_semantics=("arbitrary", "arbitrary"),
            vmem_limit_bytes=vmem_limit,
        ),
        name="ffn",
    )(x, gain, wg, wu, wd)


def _mix_in_kernel(x_hbm, gain_ref, w_ref, o_ref, xs_ref, h_ref, wb_ref, sem, *, row_chunks):
    i, j = pl.program_id(0), pl.program_id(1)

    @pl.when(i == 0)
    def _():
        wb_ref[j] = w_ref[...].astype(BF16)

    def start_row_tile(x):
        h_ref[...] = (x * _rms_scale(x) * gain_ref[...]).astype(BF16)

    _on_staged_row_tile(x_hbm, xs_ref, sem, start_row_tile)

    w = wb_ref[j]
    rc = h_ref.shape[0] // row_chunks
    for r in range(row_chunks):
        rows = slice(r * rc, (r + 1) * rc)
        o_ref[rows, :] = jnp.dot(h_ref[rows, :], w, preferred_element_type=F32)


def _mix_in(x, gain, w, *, layer, tm, tn, row_chunks):
    m, d = x.shape
    n = w.shape[2]
    nj = n // tn
    assert nj >= 2
    window_bytes = tm * d * 4 + tm * d * 2 + d * n * 2 + 2 * d * tn * 4 + 2 * tm * tn * 4
    vmem_limit = window_bytes + VMEM_SPILL_HEADROOM_BYTES
    assert vmem_limit <= VMEM_CAPACITY_BYTES
    return pl.pallas_call(
        functools.partial(_mix_in_kernel, row_chunks=row_chunks),
        grid=(m // tm, nj),
        in_specs=[
            pl.BlockSpec(memory_space=pl.ANY),
            _layer_row_spec(layer, d, 2),
            pl.BlockSpec((None, d, tn),
                         lambda i, j: (layer, 0, jnp.where(i == 0, j, nj - 1))),
        ],
        out_specs=pl.BlockSpec((tm, tn), lambda i, j: (i, j)),
        out_shape=jax.ShapeDtypeStruct((m, n), F32),
        scratch_shapes=[pltpu.VMEM((tm, d), F32), pltpu.VMEM((tm, d), BF16),
                        pltpu.VMEM((nj, d, tn), BF16), pltpu.SemaphoreType.DMA(())],
        compiler_params=pltpu.CompilerParams(
            dimension_semantics=("arbitrary", "arbitrary"),
            vmem_limit_bytes=vmem_limit,
        ),
        name="mix_in",
    )(x, gain, w)


def _rnn_kernel(x_ref, g_ref, cw_ref, cb_ref, wa_ref, ba_ref, wx_ref, bx_ref,
                lru_ref, gain_ref, o_ref, xs_ref, hprev_ref):
    ts, c = x_ref.shape

    @pl.when(pl.program_id(1) == 0)
    def _():
        xs_ref[0:SUBLANES, :] = jnp.zeros((SUBLANES, c), F32)
        hprev_ref[...] = jnp.zeros_like(hprev_ref)

    x = x_ref[...]
    xs_ref[SUBLANES:, :] = x
    xc = cb_ref[...] + x * cw_ref[CONV_W - 1:CONV_W, :]
    for k in range(CONV_W - 1):
        start = SUBLANES - (CONV_W - 1 - k)
        xc = xc + xs_ref[start:start + ts, :] * cw_ref[k:k + 1, :]
    xs_ref[0:SUBLANES, :] = x[ts - SUBLANES:, :]

    xcb = xc.astype(BF16)
    n_slabs = c // MXU_DIM

    def gate(w_ref, b_ref):
        parts = [
            jnp.dot(xcb[:, s * MXU_DIM:(s + 1) * MXU_DIM], w_ref[s],
                    preferred_element_type=F32)
            for s in range(n_slabs)
        ]
        return jax.nn.sigmoid(jnp.concatenate(parts, axis=1) + b_ref[...])

    r = gate(wa_ref, ba_ref)
    i = gate(wx_ref, bx_ref)

    neg_l = -lru_ref[...]
    softplus = jnp.maximum(neg_l, 0.0) + jnp.log1p(jnp.exp(-jnp.abs(neg_l)))
    log_a = (-LRU_C * r) * softplus
    a = jnp.exp(log_a)
    mult = jnp.sqrt(-jnp.tanh(log_a) * (1.0 + a * a))
    u = mult * (i * xc)

    sub = lax.broadcasted_iota(jnp.int32, (SUBLANES, c), 0)
    carry = hprev_ref[0:1, :]
    groups = []
    for g in range(ts // SUBLANES):
        rows = slice(g * SUBLANES, (g + 1) * SUBLANES)
        acc_a, acc_h = a[rows, :], u[rows, :]
        k = 1
        while k < SUBLANES:
            keep = sub >= k
            sh_a = pltpu.roll(acc_a, k, axis=0)
            sh_h = pltpu.roll(acc_h, k, axis=0)
            acc_h = jnp.where(keep, acc_a * sh_h + acc_h, acc_h)
            acc_a = jnp.where(keep, acc_a * sh_a, acc_a)
            k *= 2
        h_g = acc_h + acc_a * carry
        carry = h_g[SUBLANES - 1:SUBLANES, :]
        groups.append(h_g)
    h = jnp.concatenate(groups, axis=0)
    hprev_ref[0:1, :] = carry

    y = jax.nn.gelu(g_ref[...]) * h
    o_ref[...] = (y * _rms_scale(y) * gain_ref[...]).astype(o_ref.dtype)


def _rnn(p, conv_w, conv_b, wa_bd, ba, wx_bd, bx, lru, gain, *, layer, batch, seq, ts):
    c = conv_w.shape[2]
    nt = seq // ts
    row = lambda b, s: (b * nt + s, 0)
    gate_row = lambda b, s: (b * nt + s, 1)
    vec = _layer_row_spec(layer, c, 2)
    return pl.pallas_call(
        _rnn_kernel,
        grid=(batch, nt),
        in_specs=[
            pl.BlockSpec((ts, c), row),
            pl.BlockSpec((ts, c), gate_row),
            pl.BlockSpec((None, CONV_W, c), lambda b, s: (layer, 0, 0)),
            vec,
            pl.BlockSpec(wa_bd.shape, lambda b, s: (0, 0, 0)),
            vec,
            pl.BlockSpec(wx_bd.shape, lambda b, s: (0, 0, 0)),
            vec,
            vec,
            vec,
        ],
        out_specs=pl.BlockSpec((ts, c), row),
        out_shape=jax.ShapeDtypeStruct((batch * seq, c), BF16),
        scratch_shapes=[pltpu.VMEM((SUBLANES + ts, c), F32), pltpu.VMEM((SUBLANES, c), F32)],
        compiler_params=pltpu.CompilerParams(
            dimension_semantics=("arbitrary", "arbitrary"),
            vmem_limit_bytes=VMEM_LIMIT_BYTES,
        ),
        name="rnn",
    )(p, p, conv_w, conv_b, wa_bd, ba, wx_bd, bx, lru, gain)


def _qk_prep(t, gain, cos, sin_signed):
    lane = lax.broadcasted_iota(jnp.int32, t.shape, 1)
    lo = lane < ATT_DH
    t2 = t * t
    ss_lo = jnp.sum(jnp.where(lo, t2, 0.0), axis=-1, keepdims=True)
    ss_hi = jnp.sum(jnp.where(lo, 0.0, t2), axis=-1, keepdims=True)
    ms = jnp.where(lo, ss_lo, ss_hi) * (1.0 / ATT_DH)
    y = t * lax.rsqrt(ms + EPS) * gain
    half = ATT_DH // 2
    ahead = pltpu.roll(y, LANES - half, axis=1)
    behind = pltpu.roll(y, half, axis=1)
    rot = jnp.where((lane & half) == 0, ahead, behind)
    return y * cos + rot * sin_signed


def _attn_kernel(q_ref, k_ref, v_ref, cos_ref, sin_ref, qn_ref, kn_ref, lam_ref,
                 subln_ref, o_ref, kh_ref, vt_ref, *, tq, lam_init):
    seq = q_ref.shape[0]
    kh_ref[...] = _qk_prep(k_ref[...].astype(F32), kn_ref[...], cos_ref[...],
                           sin_ref[...]).astype(BF16)
    vt_ref[0:ATT_DV, :] = v_ref[...].astype(F32).T.astype(BF16)
    vt_ref[ATT_DV:, :] = jnp.ones((vt_ref.shape[0] - ATT_DV, seq), BF16)

    lp = lam_ref[...]
    lam = (jnp.exp(jnp.sum(lp[0:1] * lp[1:2], axis=-1, keepdims=True))
           - jnp.exp(jnp.sum(lp[2:3] * lp[3:4], axis=-1, keepdims=True))
           + lam_init)

    lane = lax.broadcasted_iota(jnp.int32, (tq, ATT_DV), 1)
    lo = lane < ATT_DH
    causal_t = (lax.broadcasted_iota(jnp.int32, (tq, 2 * tq), 0)
                <= (lax.broadcasted_iota(jnp.int32, (tq, 2 * tq), 1) & (tq - 1)))
    nt_dims = (((1,), (1,)), ((), ()))

    def score_phase(blk, out):
        rows = slice(blk * tq, (blk + 1) * tq)
        qh = _qk_prep(q_ref[rows, :].astype(F32), qn_ref[...], cos_ref[rows, :],
                      sin_ref[rows, :])
        qh = qh * (math.log2(math.e) / math.sqrt(ATT_DH))
        q_both = jnp.concatenate([jnp.where(lo, qh, 0.0), jnp.where(lo, 0.0, qh)],
                                 axis=0).astype(BF16)
        chunks, m = [], None
        for c in range(blk + 1):
            st = lax.dot_general(kh_ref[c * tq:(c + 1) * tq, :], q_both, nt_dims,
                                 preferred_element_type=F32)
            if c == blk:
                st = jnp.where(causal_t, st, NEG_INF)
            cm = jnp.max(st, axis=0, keepdims=True)
            m = cm if m is None else jnp.maximum(m, cm)
            chunks.append(st)
            yield
        out.append((chunks, m))

    def value_phase(blk, chunks, m):
        acc = None
        for c, st in enumerate(chunks):
            part = jnp.dot(vt_ref[:, c * tq:(c + 1) * tq], jnp.exp2(st - m).astype(BF16),
                           preferred_element_type=F32)
            acc = part if acc is None else acc + part
            yield
        l = acc[ATT_DV:ATT_DV + 1, :]
        a1, a2 = acc[0:ATT_DV, 0:tq], acc[0:ATT_DV, tq:]
        l1, l2 = l[:, 0:tq], l[:, tq:]
        ot = a1 * (1.0 / l1) - a2 * (lam / l2)
        ot = ot * lax.rsqrt(jnp.mean(ot * ot, axis=0, keepdims=True) + EPS)
        o = ot.T * (subln_ref[...] * (1.0 - lam_init))
        o_ref[blk * tq:(blk + 1) * tq, :] = o.astype(o_ref.dtype)

    n_blk = seq // tq
    scored = []
    for _ in score_phase(0, scored):
        pass
    for blk in range(n_blk):
        phases = [value_phase(blk, *scored[blk])]
        if blk + 1 < n_blk:
            phases.append(score_phase(blk + 1, scored))
        while phases:
            for ph in list(phases):
                if next(ph, StopIteration) is StopIteration:
                    phases.remove(ph)


def _attn(p, cos, sin_signed, qn, kn, lam_params, subln, *, layer, batch, seq, tq,
          lam_init, q_col, k_col, v_col):
    d = ATT_DV
    const2 = lambda b, h: (0, 0)
    vec = _layer_row_spec(layer, d, 2)
    return pl.pallas_call(
        functools.partial(_attn_kernel, tq=tq, lam_init=lam_init),
        grid=(batch, ATT_HEADS),
        in_specs=[
            pl.BlockSpec((seq, d), lambda b, h: (b, q_col + h)),
            pl.BlockSpec((seq, d), lambda b, h: (b, k_col + h)),
            pl.BlockSpec((seq, d), lambda b, h: (b, v_col + h)),
            pl.BlockSpec((seq, d), const2),
            pl.BlockSpec((seq, d), const2),
            vec,
            vec,
            pl.BlockSpec((None, 4, ATT_DH), lambda b, h: (layer, 0, 0)),
            vec,
        ],
        out_specs=pl.BlockSpec((seq, d), lambda b, h: (b, h)),
        out_shape=jax.ShapeDtypeStruct((batch * seq, ATT_HEADS * d), BF16),
        scratch_shapes=[pltpu.VMEM((seq, d), BF16), pltpu.VMEM((d + BF16_SUBLANES, seq), BF16)],
        compiler_params=pltpu.CompilerParams(
            dimension_semantics=("arbitrary", "arbitrary"),
            vmem_limit_bytes=VMEM_LIMIT_BYTES,
        ),
        name="diff_attn",
    )(p, p, p, cos, sin_signed, qn, kn, lam_params, subln)


def _out_proj_kernel(x_ref, rnn_ref, att_ref, w_ref, o_ref, wb_ref):
    @pl.when(pl.program_id(0) == 0)
    def _():
        wb_ref[...] = w_ref[...].astype(BF16)

    c = rnn_ref.shape[1]
    mixed = jnp.dot(rnn_ref[...], wb_ref[:c, :], preferred_element_type=F32)
    mixed = mixed + jnp.dot(att_ref[...], wb_ref[c:, :], preferred_element_type=F32)
    o_ref[...] = x_ref[...] + mixed


def _out_proj(x, rnn, att, w, *, layer, tm):
    m, d = x.shape
    c = rnn.shape[1]
    k = w.shape[1]
    return pl.pallas_call(
        _out_proj_kernel,
        grid=(m // tm,),
        in_specs=[
            pl.BlockSpec((tm, d), lambda i: (i, 0)),
            pl.BlockSpec((tm, c), lambda i: (i, 0)),
            pl.BlockSpec((tm, att.shape[1]), lambda i: (i, 0)),
            pl.BlockSpec((None, k, d), lambda i: (layer, 0, 0),
                         pipeline_mode=pl.Buffered(1)),
        ],
        out_specs=pl.BlockSpec((tm, d), lambda i: (i, 0)),
        out_shape=jax.ShapeDtypeStruct((m, d), F32),
        scratch_shapes=[pltpu.VMEM((k, d), BF16)],
        compiler_params=pltpu.CompilerParams(
            dimension_semantics=("arbitrary",),
            vmem_limit_bytes=VMEM_LIMIT_BYTES,
        ),
        name="out_proj",
    )(x, rnn, att, w)


def _block_diag_slabs(w):
    g, n, _ = w.shape
    per = MXU_DIM // n
    w5 = w.reshape(g // per, per, n, 1, n)
    eye = jnp.eye(per, dtype=bool)[None, :, None, :, None]
    return jnp.where(eye, w5, 0.0).reshape(g // per, MXU_DIM, MXU_DIM)


def _rope_tables(seq):
    inv_freq = ROPE_THETA ** (-jnp.arange(0, ATT_DH, 2, dtype=F32) / ATT_DH)
    ang = jnp.arange(seq, dtype=F32)[:, None] * inv_freq[None, :]
    ang = jnp.concatenate([ang, ang], axis=-1)
    cos, sin = jnp.cos(ang), jnp.sin(ang)
    half = ATT_DH // 2
    sin_signed = jnp.concatenate([-sin[:, :half], sin[:, half:]], axis=-1)
    return jnp.tile(cos, (1, 2)), jnp.tile(sin_signed, (1, 2))


def kernel(x, ffn1_norm, ffn1_w_gate, ffn1_w_up, ffn1_w_down, mix_norm, w_in, conv_w, conv_b, gate_a_w, gate_a_b, gate_x_w, gate_x_b, lru_param, rnn_out_norm, q_norm, k_norm, lam_q1, lam_k1, lam_q2, lam_k2, subln, w_out, ffn2_norm, ffn2_w_gate, ffn2_w_up, ffn2_w_down):
    bsz, seq, d = x.shape
    depth = w_in.shape[0]
    d_rnn = conv_w.shape[-1]
    d_q = ATT_HEADS * 2 * ATT_DH
    assert w_in.shape[-1] == 2 * d_rnn + 2 * d_q + ATT_HEADS * ATT_DV
    assert d_rnn % LANES == 0 and d_q % LANES == 0
    q_col = 2 * d_rnn // ATT_DV
    k_col = q_col + d_q // ATT_DV
    v_col = k_col + d_q // ATT_DV

    cos, sin_signed = _rope_tables(seq)
    xf = x.reshape(bsz * seq, d)
    rows = lambda v: v.reshape(depth, 1, -1)
    ffn1_norm, ffn2_norm, mix_norm = rows(ffn1_norm), rows(ffn2_norm), rows(mix_norm)
    conv_b, lru_param, rnn_out_norm = rows(conv_b), rows(lru_param), rows(rnn_out_norm)
    gate_a_b, gate_x_b, subln = rows(gate_a_b), rows(gate_x_b), rows(subln)
    q_gain = rows(jnp.tile(q_norm, (1, 2)))
    k_gain = rows(jnp.tile(k_norm, (1, 2)))
    lam_params = jnp.stack([lam_q1, lam_k1, lam_q2, lam_k2], axis=1)

    for l in range(depth):
        xf = _ffn(xf, ffn1_norm, ffn1_w_gate, ffn1_w_up, ffn1_w_down,
                  layer=l, tm=1024, tf=512)

        p = _mix_in(xf, mix_norm, w_in, layer=l, tm=1024, tn=512, row_chunks=4)

        out_rnn = _rnn(
            p, conv_w, conv_b,
            _block_diag_slabs(gate_a_w[l]).astype(BF16), gate_a_b,
            _block_diag_slabs(gate_x_w[l]).astype(BF16), gate_x_b,
            lru_param, rnn_out_norm, layer=l, batch=bsz, seq=seq, ts=512)

        lam_init = 0.8 - 0.6 * math.exp(-0.3 * l)
        out_att = _attn(
            p, cos, sin_signed, q_gain, k_gain, lam_params, subln,
            layer=l, batch=bsz, seq=seq, tq=256, lam_init=lam_init,
            q_col=q_col, k_col=k_col, v_col=v_col)

        xf = _out_proj(xf, out_rnn, out_att, w_out, layer=l, tm=512)

        xf = _ffn(xf, ffn2_norm, ffn2_w_gate, ffn2_w_up, ffn2_w_down,
                  layer=l, tm=1024, tf=512)
    return xf.reshape(bsz, seq, d)
```

```python
import functools
import math

import jax
import jax.numpy as jnp
from jax import lax
from jax.experimental import pallas as pl
from jax.experimental.pallas import tpu as pltpu

RNN_BLOCKS = 16
CONV_W = 4
LRU_C = 8.0
ATT_HEADS = 8
ATT_DH = 64
ATT_DV = 2 * ATT_DH
ROPE_THETA = 10000.0
EPS = 1e-6
NEG_INF = -1e30

LANES = 128
SUBLANES = 8
BF16_SUBLANES = 16
MXU_DIM = 256
VMEM_CAPACITY_BYTES = 64 * 1024 * 1024
VMEM_LIMIT_BYTES = 58 * 1024 * 1024
VMEM_SPILL_HEADROOM_BYTES = 8 * 1024 * 1024

F32 = jnp.float32
BF16 = jnp.bfloat16


def _rms_scale(x):
    return lax.rsqrt(jnp.mean(x * x, axis=-1, keepdims=True) + EPS)


def _layer_row_spec(layer, width, n_grid):
    zeros = (0,) * 2
    if n_grid == 1:
        return pl.BlockSpec((None, 1, width), lambda i: (layer, *zeros))
    if n_grid == 2:
        return pl.BlockSpec((None, 1, width), lambda i, j: (layer, *zeros))
    return pl.BlockSpec((None, 1, width), lambda i, j, k: (layer, *zeros))


def _on_staged_row_tile(x_hbm, xs_ref, sem, first_step, later_step):
    i, j = pl.program_id(0), pl.program_id(1)
    tm = xs_ref.shape[0]

    def x_tile_copy(tile):
        start = pl.multiple_of(tile * tm, tm)
        return pltpu.make_async_copy(x_hbm.at[pl.ds(start, tm), :], xs_ref, sem)

    @pl.when((i == 0) & (j == 0))
    def _():
        x_tile_copy(0).start()

    @pl.when(j == 0)
    def _():
        x_tile_copy(i).wait()
        first_step(xs_ref[...])

    @pl.when((j == 1) & (i + 1 < pl.num_programs(0)))
    def _():
        x_tile_copy(i + 1).start()

    @pl.when(j > 0)
    def _():
        later_step()


def _ffn_kernel(x_hbm, gain_ref, wg_ref, wu_ref, wd_ref, o_ref, xs_ref, h_ref, sem):
    def hidden_chunk():
        h = h_ref[...]
        g = jnp.dot(h, wg_ref[...].astype(BF16), preferred_element_type=F32)
        u = jnp.dot(h, wu_ref[...].astype(BF16), preferred_element_type=F32)
        a = (g * jax.nn.sigmoid(g)) * u * 0.5
        return jnp.dot(a.astype(BF16), wd_ref[...].astype(BF16), preferred_element_type=F32)

    def first_step(x):
        h_ref[...] = (x * _rms_scale(x) * gain_ref[...]).astype(BF16)
        o_ref[...] = x + hidden_chunk()

    def later_step():
        o_ref[...] += hidden_chunk()

    _on_staged_row_tile(x_hbm, xs_ref, sem, first_step, later_step)


def _ffn(x, gain, wg, wu, wd, *, layer, tm, tf):
    m, d = x.shape
    f = wg.shape[2]
    assert f // tf >= 2
    window_bytes = 2 * tm * d * 4 + 2 * 3 * d * tf * 4 + tm * d * 4 + tm * d * 2
    vmem_limit = window_bytes + VMEM_SPILL_HEADROOM_BYTES
    assert vmem_limit <= VMEM_CAPACITY_BYTES
    return pl.pallas_call(
        _ffn_kernel,
        grid=(m // tm, f // tf),
        in_specs=[
            pl.BlockSpec(memory_space=pl.ANY),
            _layer_row_spec(layer, d, 2),
            pl.BlockSpec((None, d, tf), lambda i, j: (layer, 0, j)),
            pl.BlockSpec((None, d, tf), lambda i, j: (layer, 0, j)),
            pl.BlockSpec((None, tf, d), lambda i, j: (layer, j, 0)),
        ],
        out_specs=pl.BlockSpec((tm, d), lambda i, j: (i, 0)),
        out_shape=jax.ShapeDtypeStruct((m, d), F32),
        scratch_shapes=[pltpu.VMEM((tm, d), F32), pltpu.VMEM((tm, d), BF16),
                        pltpu.SemaphoreType.DMA(())],
        compiler_params=pltpu.CompilerParams(
            dimension_semantics=("arbitrary", "arbitrary"),
            vmem_limit_bytes=vmem_limit,
        ),
        name="ffn",
    )(x, gain, wg, wu, wd)


def _mix_in_kernel(x_hbm, gain_ref, w_ref, o_ref, xs_ref, h_ref, wb_ref, sem, *, row_chunks):
    i, j = pl.program_id(0), pl.program_id(1)

    @pl.when(i == 0)
    def _():
        wb_ref[j] = w_ref[...].astype(BF16)

    rc = h_ref.shape[0] // row_chunks

    def column_step(x):
        w = wb_ref[j]
        for r in range(row_chunks):
            rows = slice(r * rc, (r + 1) * rc)
            if x is not None:
                xr = x[rows, :]
                h_ref[rows, :] = (xr * _rms_scale(xr) * gain_ref[...]).astype(BF16)
            o_ref[rows, :] = jnp.dot(h_ref[rows, :], w, preferred_element_type=F32)

    _on_staged_row_tile(x_hbm, xs_ref, sem, column_step, functools.partial(column_step, None))


def _mix_in(x, gain, w, *, layer, tm, tn, row_chunks):
    m, d = x.shape
    n = w.shape[2]
    nj = n // tn
    assert nj >= 2
    window_bytes = tm * d * 4 + tm * d * 2 + d * n * 2 + 2 * d * tn * 4 + 2 * tm * tn * 4
    vmem_limit = window_bytes + VMEM_SPILL_HEADROOM_BYTES
    assert vmem_limit <= VMEM_CAPACITY_BYTES
    return pl.pallas_call(
        functools.partial(_mix_in_kernel, row_chunks=row_chunks),
        grid=(m // tm, nj),
        in_specs=[
            pl.BlockSpec(memory_space=pl.ANY),
            _layer_row_spec(layer, d, 2),
            pl.BlockSpec((None, d, tn),
                         lambda i, j: (layer, 0, jnp.where(i == 0, j, nj - 1))),
        ],
        out_specs=pl.BlockSpec((tm, tn), lambda i, j: (i, j)),
        out_shape=jax.ShapeDtypeStruct((m, n), F32),
        scratch_shapes=[pltpu.VMEM((tm, d), F32), pltpu.VMEM((tm, d), BF16),
                        pltpu.VMEM((nj, d, tn), BF16), pltpu.SemaphoreType.DMA(())],
        compiler_params=pltpu.CompilerParams(
            dimension_semantics=("arbitrary", "arbitrary"),
            vmem_limit_bytes=vmem_limit,
        ),
        name="mix_in",
    )(x, gain, w)


def _rnn_kernel(x_ref, g_ref, cw_ref, cb_ref, wa_ref, ba_ref, wx_ref, bx_ref,
                lru_ref, gain_ref, o_ref, xs_ref, hprev_ref):
    ts, c = x_ref.shape

    @pl.when(pl.program_id(1) == 0)
    def _():
        xs_ref[0:SUBLANES, :] = jnp.zeros((SUBLANES, c), F32)
        hprev_ref[...] = jnp.zeros_like(hprev_ref)

    x = x_ref[...]
    xs_ref[SUBLANES:, :] = x
    xc = cb_ref[...] + x * cw_ref[CONV_W - 1:CONV_W, :]
    for k in range(CONV_W - 1):
        start = SUBLANES - (CONV_W - 1 - k)
        xc = xc + xs_ref[start:start + ts, :] * cw_ref[k:k + 1, :]
    xs_ref[0:SUBLANES, :] = x[ts - SUBLANES:, :]

    xcb = xc.astype(BF16)
    n_slabs = c // MXU_DIM

    def gate(w_ref, b_ref):
        parts = [
            jnp.dot(xcb[:, s * MXU_DIM:(s + 1) * MXU_DIM], w_ref[s],
                    preferred_element_type=F32)
            for s in range(n_slabs)
        ]
        return jax.nn.sigmoid(jnp.concatenate(parts, axis=1) + b_ref[...])

    r = gate(wa_ref, ba_ref)
    i = gate(wx_ref, bx_ref)

    neg_l = -lru_ref[...]
    softplus = jnp.maximum(neg_l, 0.0) + jnp.log1p(jnp.exp(-jnp.abs(neg_l)))
    log_a = (-LRU_C * r) * softplus
    a = jnp.exp(log_a)
    mult = jnp.sqrt(-jnp.tanh(log_a) * (1.0 + a * a))
    u = mult * (i * xc)

    sub = lax.broadcasted_iota(jnp.int32, (SUBLANES, c), 0)
    carry = hprev_ref[0:1, :]
    groups = []
    for g in range(ts // SUBLANES):
        rows = slice(g * SUBLANES, (g + 1) * SUBLANES)
        acc_a, acc_h = a[rows, :], u[rows, :]
        k = 1
        while k < SUBLANES:
            keep = sub >= k
            sh_a = pltpu.roll(acc_a, k, axis=0)
            sh_h = pltpu.roll(acc_h, k, axis=0)
            acc_h = jnp.where(keep, acc_a * sh_h + acc_h, acc_h)
            acc_a = jnp.where(keep, acc_a * sh_a, acc_a)
            k *= 2
        h_g = acc_h + acc_a * carry
        carry = h_g[SUBLANES - 1:SUBLANES, :]
        groups.append(h_g)
    h = jnp.concatenate(groups, axis=0)
    hprev_ref[0:1, :] = carry

    y = jax.nn.gelu(g_ref[...]) * h
    o_ref[...] = (y * _rms_scale(y) * gain_ref[...]).astype(o_ref.dtype)


def _rnn(p, conv_w, conv_b, wa_bd, ba, wx_bd, bx, lru, gain, *, layer, batch, seq, ts):
    c = conv_w.shape[2]
    nt = seq // ts
    row = lambda b, s: (b * nt + s, 0)
    gate_row = lambda b, s: (b * nt + s, 1)
    vec = _layer_row_spec(layer, c, 2)
    return pl.pallas_call(
        _rnn_kernel,
        grid=(batch, nt),
        in_specs=[
            pl.BlockSpec((ts, c), row),
            pl.BlockSpec((ts, c), gate_row),
            pl.BlockSpec((None, CONV_W, c), lambda b, s: (layer, 0, 0)),
            vec,
            pl.BlockSpec(wa_bd.shape, lambda b, s: (0, 0, 0)),
            vec,
            pl.BlockSpec(wx_bd.shape, lambda b, s: (0, 0, 0)),
            vec,
            vec,
            vec,
        ],
        out_specs=pl.BlockSpec((ts, c), row),
        out_shape=jax.ShapeDtypeStruct((batch * seq, c), BF16),
        scratch_shapes=[pltpu.VMEM((SUBLANES + ts, c), F32), pltpu.VMEM((SUBLANES, c), F32)],
        compiler_params=pltpu.CompilerParams(
            dimension_semantics=("arbitrary", "arbitrary"),
            vmem_limit_bytes=VMEM_LIMIT_BYTES,
        ),
        name="rnn",
    )(p, p, conv_w, conv_b, wa_bd, ba, wx_bd, bx, lru, gain)


def _qk_prep(t, gain, cos, sin_signed):
    lane = lax.broadcasted_iota(jnp.int32, t.shape, 1)
    lo = lane < ATT_DH
    t2 = t * t
    ss_lo = jnp.sum(jnp.where(lo, t2, 0.0), axis=-1, keepdims=True)
    ss_hi = jnp.sum(jnp.where(lo, 0.0, t2), axis=-1, keepdims=True)
    ms = jnp.where(lo, ss_lo, ss_hi) * (1.0 / ATT_DH)
    y = t * lax.rsqrt(ms + EPS) * gain
    half = ATT_DH // 2
    ahead = pltpu.roll(y, LANES - half, axis=1)
    behind = pltpu.roll(y, half, axis=1)
    rot = jnp.where((lane & half) == 0, ahead, behind)
    return y * cos + rot * sin_signed


def _attn_head(q_ref, k_ref, v_ref, cos_ref, sin_ref, qn_ref, kn_ref, lam_ref,
               subln_ref, o_ref, kh_ref, vt_ref, *, tq, lam_init):
    seq = q_ref.shape[0]
    kh_ref[...] = _qk_prep(k_ref[...].astype(F32), kn_ref[...], cos_ref[...],
                           sin_ref[...]).astype(BF16)
    vt_ref[0:ATT_DV, :] = v_ref[...].astype(F32).T.astype(BF16)
    vt_ref[ATT_DV:, :] = jnp.ones((vt_ref.shape[0] - ATT_DV, seq), BF16)

    lp = lam_ref[...]
    lam = (jnp.exp(jnp.sum(lp[0:1] * lp[1:2], axis=-1, keepdims=True))
           - jnp.exp(jnp.sum(lp[2:3] * lp[3:4], axis=-1, keepdims=True))
           + lam_init)

    lane = lax.broadcasted_iota(jnp.int32, (tq, ATT_DV), 1)
    lo = lane < ATT_DH
    causal_t = (lax.broadcasted_iota(jnp.int32, (tq, 2 * tq), 0)
                <= (lax.broadcasted_iota(jnp.int32, (tq, 2 * tq), 1) & (tq - 1)))
    nt_dims = (((1,), (1,)), ((), ()))

    def score_phase(blk, out):
        rows = slice(blk * tq, (blk + 1) * tq)
        qh = _qk_prep(q_ref[rows, :].astype(F32), qn_ref[...], cos_ref[rows, :],
                      sin_ref[rows, :])
        qh = qh * (math.log2(math.e) / math.sqrt(ATT_DH))
        q_both = jnp.concatenate([jnp.where(lo, qh, 0.0), jnp.where(lo, 0.0, qh)],
                                 axis=0).astype(BF16)
        chunks, m = [], None
        for c in range(blk + 1):
            st = lax.dot_general(kh_ref[c * tq:(c + 1) * tq, :], q_both, nt_dims,
                                 preferred_element_type=F32)
            if c == blk:
                st = jnp.where(causal_t, st, NEG_INF)
            cm = jnp.max(st, axis=0, keepdims=True)
            m = cm if m is None else jnp.maximum(m, cm)
            chunks.append(st)
            yield
        out.append((chunks, m))

    def value_phase(blk, chunks, m):
        acc = None
        for c, st in enumerate(chunks):
            part = jnp.dot(vt_ref[:, c * tq:(c + 1) * tq], jnp.exp2(st - m).astype(BF16),
                           preferred_element_type=F32)
            acc = part if acc is None else acc + part
            yield
        l = acc[ATT_DV:ATT_DV + 1, :]
        a1, a2 = acc[0:ATT_DV, 0:tq], acc[0:ATT_DV, tq:]
        l1, l2 = l[:, 0:tq], l[:, tq:]
        ot = a1 * (1.0 / l1) - a2 * (lam / l2)
        ot = ot * lax.rsqrt(jnp.mean(ot * ot, axis=0, keepdims=True) + EPS)
        o = ot.T * (subln_ref[...] * (1.0 - lam_init))
        o_ref[blk * tq:(blk + 1) * tq, :] = o.astype(o_ref.dtype)

    n_blk = seq // tq
    scored = []
    for _ in score_phase(0, scored):
        pass
    for blk in range(n_blk):
        phases = [value_phase(blk, *scored[blk])]
        if blk + 1 < n_blk:
            phases.append(score_phase(blk + 1, scored))
        while phases:
            for ph in list(phases):
                if next(ph, StopIteration) is StopIteration:
                    phases.remove(ph)


def _attn_kernel(q_ref, k_ref, v_ref, cos_ref, sin_ref, qn_ref, kn_ref, lam_ref,
                 subln_ref, o_ref, kh_ref, vt_ref, *, tq, lam_init):
    for hh in range(kh_ref.shape[0]):
        cols = slice(hh * ATT_DV, (hh + 1) * ATT_DV)
        _attn_head(q_ref.at[:, cols], k_ref.at[:, cols], v_ref.at[:, cols], cos_ref, sin_ref,
                   qn_ref, kn_ref, lam_ref, subln_ref, o_ref.at[:, cols], kh_ref.at[hh],
                   vt_ref.at[hh], tq=tq, lam_init=lam_init)


def _attn(p, cos, sin_signed, qn, kn, lam_params, subln, *, layer, batch, seq, tq,
          lam_init, q_col, k_col, v_col, heads_per_step):
    d = ATT_DV
    hp = heads_per_step
    assert ATT_HEADS % hp == 0 and q_col % hp == 0 and k_col % hp == 0 and v_col % hp == 0
    const2 = lambda b, h: (0, 0)
    vec = _layer_row_spec(layer, d, 2)
    return pl.pallas_call(
        functools.partial(_attn_kernel, tq=tq, lam_init=lam_init),
        grid=(batch, ATT_HEADS // hp),
        in_specs=[
            pl.BlockSpec((seq, hp * d), lambda b, h: (b, q_col // hp + h)),
            pl.BlockSpec((seq, hp * d), lambda b, h: (b, k_col // hp + h)),
            pl.BlockSpec((seq, hp * d), lambda b, h: (b, v_col // hp + h)),
            pl.BlockSpec((seq, d), const2),
            pl.BlockSpec((seq, d), const2),
            vec,
            vec,
            pl.BlockSpec((None, 4, ATT_DH), lambda b, h: (layer, 0, 0)),
            vec,
        ],
        out_specs=pl.BlockSpec((seq, hp * d), lambda b, h: (b, h)),
        out_shape=jax.ShapeDtypeStruct((batch * seq, ATT_HEADS * d), BF16),
        scratch_shapes=[pltpu.VMEM((hp, seq, d), BF16),
                        pltpu.VMEM((hp, d + BF16_SUBLANES, seq), BF16)],
        compiler_params=pltpu.CompilerParams(
            dimension_semantics=("arbitrary", "arbitrary"),
            vmem_limit_bytes=VMEM_LIMIT_BYTES,
        ),
        name="diff_attn",
    )(p, p, p, cos, sin_signed, qn, kn, lam_params, subln)


def _out_proj_kernel(x_ref, rnn_ref, att_ref, w_ref, o_ref, wb_ref):
    @pl.when(pl.program_id(0) == 0)
    def _():
        wb_ref[...] = w_ref[...].astype(BF16)

    c = rnn_ref.shape[1]
    mixed = jnp.dot(rnn_ref[...], wb_ref[:c, :], preferred_element_type=F32)
    mixed = mixed + jnp.dot(att_ref[...], wb_ref[c:, :], preferred_element_type=F32)
    o_ref[...] = x_ref[...] + mixed


def _out_proj(x, rnn, att, w, *, layer, tm):
    m, d = x.shape
    c = rnn.shape[1]
    k = w.shape[1]
    return pl.pallas_call(
        _out_proj_kernel,
        grid=(m // tm,),
        in_specs=[
            pl.BlockSpec((tm, d), lambda i: (i, 0)),
            pl.BlockSpec((tm, c), lambda i: (i, 0)),
            pl.BlockSpec((tm, att.shape[1]), lambda i: (i, 0)),
            pl.BlockSpec((None, k, d), lambda i: (layer, 0, 0),
                         pipeline_mode=pl.Buffered(1)),
        ],
        out_specs=pl.BlockSpec((tm, d), lambda i: (i, 0)),
        out_shape=jax.ShapeDtypeStruct((m, d), F32),
        scratch_shapes=[pltpu.VMEM((k, d), BF16)],
        compiler_params=pltpu.CompilerParams(
            dimension_semantics=("arbitrary",),
            vmem_limit_bytes=VMEM_LIMIT_BYTES,
        ),
        name="out_proj",
    )(x, rnn, att, w)


def _block_diag_slabs(w):
    g, n, _ = w.shape
    per = MXU_DIM // n
    w5 = w.reshape(g // per, per, n, 1, n)
    eye = jnp.eye(per, dtype=bool)[None, :, None, :, None]
    return jnp.where(eye, w5, 0.0).reshape(g // per, MXU_DIM, MXU_DIM)


def _rope_tables(seq):
    inv_freq = ROPE_THETA ** (-jnp.arange(0, ATT_DH, 2, dtype=F32) / ATT_DH)
    ang = jnp.arange(seq, dtype=F32)[:, None] * inv_freq[None, :]
    ang = jnp.concatenate([ang, ang], axis=-1)
    cos, sin = jnp.cos(ang), jnp.sin(ang)
    half = ATT_DH // 2
    sin_signed = jnp.concatenate([-sin[:, :half], sin[:, half:]], axis=-1)
    return jnp.tile(cos, (1, 2)), jnp.tile(sin_signed, (1, 2))


def kernel(x, ffn1_norm, ffn1_w_gate, ffn1_w_up, ffn1_w_down, mix_norm, w_in, conv_w, conv_b, gate_a_w, gate_a_b, gate_x_w, gate_x_b, lru_param, rnn_out_norm, q_norm, k_norm, lam_q1, lam_k1, lam_q2, lam_k2, subln, w_out, ffn2_norm, ffn2_w_gate, ffn2_w_up, ffn2_w_down):
    bsz, seq, d = x.shape
    depth = w_in.shape[0]
    d_rnn = conv_w.shape[-1]
    d_q = ATT_HEADS * 2 * ATT_DH
    assert w_in.shape[-1] == 2 * d_rnn + 2 * d_q + ATT_HEADS * ATT_DV
    assert d_rnn % LANES == 0 and d_q % LANES == 0
    q_col = 2 * d_rnn // ATT_DV
    k_col = q_col + d_q // ATT_DV
    v_col = k_col + d_q // ATT_DV

    cos, sin_signed = _rope_tables(seq)
    xf = x.reshape(bsz * seq, d)
    rows = lambda v: v.reshape(depth, 1, -1)
    ffn1_norm, ffn2_norm, mix_norm = rows(ffn1_norm), rows(ffn2_norm), rows(mix_norm)
    conv_b, lru_param, rnn_out_norm = rows(conv_b), rows(lru_param), rows(rnn_out_norm)
    gate_a_b, gate_x_b, subln = rows(gate_a_b), rows(gate_x_b), rows(subln)
    q_gain = rows(jnp.tile(q_norm, (1, 2)))
    k_gain = rows(jnp.tile(k_norm, (1, 2)))
    lam_params = jnp.stack([lam_q1, lam_k1, lam_q2, lam_k2], axis=1)

    for l in range(depth):
        xf = _ffn(xf, ffn1_norm, ffn1_w_gate, ffn1_w_up, ffn1_w_down,
                  layer=l, tm=1024, tf=512)

        p = _mix_in(xf, mix_norm, w_in, layer=l, tm=1024, tn=1024, row_chunks=4)

        out_rnn = _rnn(
            p, conv_w, conv_b,
            _block_diag_slabs(gate_a_w[l]).astype(BF16), gate_a_b,
            _block_diag_slabs(gate_x_w[l]).astype(BF16), gate_x_b,
            lru_param, rnn_out_norm, layer=l, batch=bsz, seq=seq, ts=512)

        lam_init = 0.8 - 0.6 * math.exp(-0.3 * l)
        out_att = _attn(
            p, cos, sin_signed, q_gain, k_gain, lam_params, subln,
            layer=l, batch=bsz, seq=seq, tq=256, lam_init=lam_init,
            q_col=q_col, k_col=k_col, v_col=v_col, heads_per_step=2)

        xf = _out_proj(xf, out_rnn, out_att, w_out, layer=l, tm=512)

        xf = _ffn(xf, ffn2_norm, ffn2_w_gate, ffn2_w_up, ffn2_w_down,
                  layer=l, tm=1024, tf=512)
    return xf.reshape(bsz, seq, d)
```

```python
import functools
import math

import jax
import jax.numpy as jnp
from jax import lax
from jax.experimental import pallas as pl
from jax.experimental.pallas import tpu as pltpu

RNN_BLOCKS = 16
CONV_W = 4
LRU_C = 8.0
ATT_HEADS = 8
ATT_DH = 64
ATT_DV = 2 * ATT_DH
ROPE_THETA = 10000.0
EPS = 1e-6
NEG_INF = -1e30

LANES = 128
SUBLANES = 8
BF16_SUBLANES = 16
MXU_DIM = 256
VMEM_CAPACITY_BYTES = 64 * 1024 * 1024
VMEM_LIMIT_BYTES = 58 * 1024 * 1024
VMEM_SPILL_HEADROOM_BYTES = 8 * 1024 * 1024

F32 = jnp.float32
BF16 = jnp.bfloat16


def _rms_scale(x):
    return lax.rsqrt(jnp.mean(x * x, axis=-1, keepdims=True) + EPS)


def _layer_row_spec(layer, width, n_grid):
    zeros = (0,) * 2
    if n_grid == 1:
        return pl.BlockSpec((None, 1, width), lambda i: (layer, *zeros))
    if n_grid == 2:
        return pl.BlockSpec((None, 1, width), lambda i, j: (layer, *zeros))
    return pl.BlockSpec((None, 1, width), lambda i, j, k: (layer, *zeros))


def _on_staged_row_tile(x_hbm, xs_ref, sem, first_step, later_step):
    i, j = pl.program_id(0), pl.program_id(1)
    tm = xs_ref.shape[0]

    def x_tile_copy(tile):
        start = pl.multiple_of(tile * tm, tm)
        return pltpu.make_async_copy(x_hbm.at[pl.ds(start, tm), :], xs_ref, sem)

    @pl.when((i == 0) & (j == 0))
    def _():
        x_tile_copy(0).start()

    @pl.when(j == 0)
    def _():
        x_tile_copy(i).wait()
        first_step(xs_ref[...])

    @pl.when((j == 1) & (i + 1 < pl.num_programs(0)))
    def _():
        x_tile_copy(i + 1).start()

    @pl.when(j > 0)
    def _():
        later_step()


def _ffn_kernel(x_hbm, gain_ref, wg_ref, wu_ref, wd_ref, o_ref, xs_ref, h_ref, sem):
    def hidden_chunk():
        h = h_ref[...]
        g = jnp.dot(h, wg_ref[...].astype(BF16), preferred_element_type=F32)
        u = jnp.dot(h, wu_ref[...].astype(BF16), preferred_element_type=F32)
        a = (g * jax.nn.sigmoid(g)) * u * 0.5
        return jnp.dot(a.astype(BF16), wd_ref[...].astype(BF16), preferred_element_type=F32)

    def first_step(x):
        h_ref[...] = (x * _rms_scale(x) * gain_ref[...]).astype(BF16)
        o_ref[...] = x + hidden_chunk()

    def later_step():
        o_ref[...] += hidden_chunk()

    _on_staged_row_tile(x_hbm, xs_ref, sem, first_step, later_step)


def _ffn(x, gain, wg, wu, wd, *, layer, tm, tf):
    m, d = x.shape
    f = wg.shape[2]
    assert f // tf >= 2
    window_bytes = 2 * tm * d * 4 + 2 * 3 * d * tf * 4 + tm * d * 4 + tm * d * 2
    vmem_limit = window_bytes + VMEM_SPILL_HEADROOM_BYTES
    assert vmem_limit <= VMEM_CAPACITY_BYTES
    return pl.pallas_call(
        _ffn_kernel,
        grid=(m // tm, f // tf),
        in_specs=[
            pl.BlockSpec(memory_space=pl.ANY),
            _layer_row_spec(layer, d, 2),
            pl.BlockSpec((None, d, tf), lambda i, j: (layer, 0, j)),
            pl.BlockSpec((None, d, tf), lambda i, j: (layer, 0, j)),
            pl.BlockSpec((None, tf, d), lambda i, j: (layer, j, 0)),
        ],
        out_specs=pl.BlockSpec((tm, d), lambda i, j: (i, 0)),
        out_shape=jax.ShapeDtypeStruct((m, d), F32),
        scratch_shapes=[pltpu.VMEM((tm, d), F32), pltpu.VMEM((tm, d), BF16),
                        pltpu.SemaphoreType.DMA(())],
        compiler_params=pltpu.CompilerParams(
            dimension_semantics=("arbitrary", "arbitrary"),
            vmem_limit_bytes=vmem_limit,
        ),
        name="ffn",
    )(x, gain, wg, wu, wd)


def _mix_in_kernel(x_hbm, gain_ref, w_ref, o_ref, xs_ref, h_ref, wb_ref, sem, *, row_chunks):
    i, j = pl.program_id(0), pl.program_id(1)

    @pl.when(i == 0)
    def _():
        wb_ref[j] = w_ref[...].astype(BF16)

    rc = h_ref.shape[0] // row_chunks

    def column_step(x):
        w = wb_ref[j]
        for r in range(row_chunks):
            rows = slice(r * rc, (r + 1) * rc)
            if x is not None:
                xr = x[rows, :]
                h_ref[rows, :] = (xr * _rms_scale(xr) * gain_ref[...]).astype(BF16)
            o_ref[rows, :] = jnp.dot(h_ref[rows, :], w, preferred_element_type=F32)

    _on_staged_row_tile(x_hbm, xs_ref, sem, column_step, functools.partial(column_step, None))


def _mix_in(x, gain, w, *, layer, tm, tn, row_chunks):
    m, d = x.shape
    n = w.shape[2]
    nj = n // tn
    assert nj >= 2
    window_bytes = tm * d * 4 + tm * d * 2 + d * n * 2 + 2 * d * tn * 4 + 2 * tm * tn * 4
    vmem_limit = window_bytes + VMEM_SPILL_HEADROOM_BYTES
    assert vmem_limit <= VMEM_CAPACITY_BYTES
    return pl.pallas_call(
        functools.partial(_mix_in_kernel, row_chunks=row_chunks),
        grid=(m // tm, nj),
        in_specs=[
            pl.BlockSpec(memory_space=pl.ANY),
            _layer_row_spec(layer, d, 2),
            pl.BlockSpec((None, d, tn),
                         lambda i, j: (layer, 0, jnp.where(i == 0, j, nj - 1))),
        ],
        out_specs=pl.BlockSpec((tm, tn), lambda i, j: (i, j)),
        out_shape=jax.ShapeDtypeStruct((m, n), F32),
        scratch_shapes=[pltpu.VMEM((tm, d), F32), pltpu.VMEM((tm, d), BF16),
                        pltpu.VMEM((nj, d, tn), BF16), pltpu.SemaphoreType.DMA(())],
        compiler_params=pltpu.CompilerParams(
            dimension_semantics=("arbitrary", "arbitrary"),
            vmem_limit_bytes=vmem_limit,
        ),
        name="mix_in",
    )(x, gain, w)


def _rnn_kernel(x_ref, g_ref, cw_ref, cb_ref, wa_ref, ba_ref, wx_ref, bx_ref,
                lru_ref, gain_ref, o_ref, xs_ref, hprev_ref):
    ts, c = x_ref.shape

    @pl.when(pl.program_id(1) == 0)
    def _():
        xs_ref[0:SUBLANES, :] = jnp.zeros((SUBLANES, c), F32)
        hprev_ref[...] = jnp.zeros_like(hprev_ref)

    x = x_ref[...]
    xs_ref[SUBLANES:, :] = x
    xc = cb_ref[...] + x * cw_ref[CONV_W - 1:CONV_W, :]
    for k in range(CONV_W - 1):
        start = SUBLANES - (CONV_W - 1 - k)
        xc = xc + xs_ref[start:start + ts, :] * cw_ref[k:k + 1, :]
    xs_ref[0:SUBLANES, :] = x[ts - SUBLANES:, :]

    xcb = xc.astype(BF16)
    n_slabs = c // MXU_DIM

    def gate(w_ref, b_ref):
        parts = [
            jnp.dot(xcb[:, s * MXU_DIM:(s + 1) * MXU_DIM], w_ref[s],
                    preferred_element_type=F32)
            for s in range(n_slabs)
        ]
        return jax.nn.sigmoid(jnp.concatenate(parts, axis=1) + b_ref[...])

    r = gate(wa_ref, ba_ref)
    i = gate(wx_ref, bx_ref)

    neg_l = -lru_ref[...]
    softplus = jnp.maximum(neg_l, 0.0) + jnp.log1p(jnp.exp(-jnp.abs(neg_l)))
    log_a = (-LRU_C * r) * softplus
    a = jnp.exp(log_a)
    mult = jnp.sqrt(-jnp.tanh(log_a) * (1.0 + a * a))
    u = mult * (i * xc)

    sub = lax.broadcasted_iota(jnp.int32, (SUBLANES, c), 0)
    carry = hprev_ref[0:1, :]
    groups = []
    for g in range(ts // SUBLANES):
        rows = slice(g * SUBLANES, (g + 1) * SUBLANES)
        acc_a, acc_h = a[rows, :], u[rows, :]
        k = 1
        while k < SUBLANES:
            keep = sub >= k
            sh_a = pltpu.roll(acc_a, k, axis=0)
            sh_h = pltpu.roll(acc_h, k, axis=0)
            acc_h = jnp.where(keep, acc_a * sh_h + acc_h, acc_h)
            acc_a = jnp.where(keep, acc_a * sh_a, acc_a)
            k *= 2
        h_g = acc_h + acc_a * carry
        carry = h_g[SUBLANES - 1:SUBLANES, :]
        groups.append(h_g)
    h = jnp.concatenate(groups, axis=0)
    hprev_ref[0:1, :] = carry

    y = jax.nn.gelu(g_ref[...]) * h
    o_ref[...] = (y * _rms_scale(y) * gain_ref[...]).astype(o_ref.dtype)


def _rnn(p, conv_w, conv_b, wa_bd, ba, wx_bd, bx, lru, gain, *, layer, batch, seq, ts):
    c = conv_w.shape[2]
    nt = seq // ts
    row = lambda b, s: (b * nt + s, 0)
    gate_row = lambda b, s: (b * nt + s, 1)
    vec = _layer_row_spec(layer, c, 2)
    return pl.pallas_call(
        _rnn_kernel,
        grid=(batch, nt),
        in_specs=[
            pl.BlockSpec((ts, c), row),
            pl.BlockSpec((ts, c), gate_row),
            pl.BlockSpec((None, CONV_W, c), lambda b, s: (layer, 0, 0)),
            vec,
            pl.BlockSpec(wa_bd.shape, lambda b, s: (0, 0, 0)),
            vec,
            pl.BlockSpec(wx_bd.shape, lambda b, s: (0, 0, 0)),
            vec,
            vec,
            vec,
        ],
        out_specs=pl.BlockSpec((ts, c), row),
        out_shape=jax.ShapeDtypeStruct((batch * seq, c), BF16),
        scratch_shapes=[pltpu.VMEM((SUBLANES + ts, c), F32), pltpu.VMEM((SUBLANES, c), F32)],
        compiler_params=pltpu.CompilerParams(
            dimension_semantics=("arbitrary", "arbitrary"),
            vmem_limit_bytes=VMEM_LIMIT_BYTES,
        ),
        name="rnn",
    )(p, p, conv_w, conv_b, wa_bd, ba, wx_bd, bx, lru, gain)


def _qk_prep(t, gain, cos, sin_signed):
    lane = lax.broadcasted_iota(jnp.int32, t.shape, 1)
    lo = lane < ATT_DH
    t2 = t * t
    ss_lo = jnp.sum(jnp.where(lo, t2, 0.0), axis=-1, keepdims=True)
    ss_hi = jnp.sum(jnp.where(lo, 0.0, t2), axis=-1, keepdims=True)
    ms = jnp.where(lo, ss_lo, ss_hi) * (1.0 / ATT_DH)
    y = t * lax.rsqrt(ms + EPS) * gain
    half = ATT_DH // 2
    ahead = pltpu.roll(y, LANES - half, axis=1)
    behind = pltpu.roll(y, half, axis=1)
    rot = jnp.where((lane & half) == 0, ahead, behind)
    return y * cos + rot * sin_signed


def _attn_head(q_ref, k_ref, v_ref, cos_ref, sin_ref, qn_ref, kn_ref, lam_ref,
               subln_ref, o_ref, kh_ref, vt_ref, *, tq, lam_init):
    seq = q_ref.shape[0]
    kh_ref[...] = _qk_prep(k_ref[...].astype(F32), kn_ref[...], cos_ref[...],
                           sin_ref[...]).astype(BF16)
    vt_ref[0:ATT_DV, :] = v_ref[...].astype(F32).T.astype(BF16)
    vt_ref[ATT_DV:, :] = jnp.ones((vt_ref.shape[0] - ATT_DV, seq), BF16)

    lp = lam_ref[...]
    lam = (jnp.exp(jnp.sum(lp[0:1] * lp[1:2], axis=-1, keepdims=True))
           - jnp.exp(jnp.sum(lp[2:3] * lp[3:4], axis=-1, keepdims=True))
           + lam_init)

    lane = lax.broadcasted_iota(jnp.int32, (tq, ATT_DV), 1)
    lo = lane < ATT_DH
    causal_t = (lax.broadcasted_iota(jnp.int32, (tq, 2 * tq), 0)
                <= (lax.broadcasted_iota(jnp.int32, (tq, 2 * tq), 1) & (tq - 1)))
    nt_dims = (((1,), (1,)), ((), ()))

    def score_phase(blk, out):
        rows = slice(blk * tq, (blk + 1) * tq)
        qh = _qk_prep(q_ref[rows, :].astype(F32), qn_ref[...], cos_ref[rows, :],
                      sin_ref[rows, :])
        qh = qh * (math.log2(math.e) / math.sqrt(ATT_DH))
        q_both = jnp.concatenate([jnp.where(lo, qh, 0.0), jnp.where(lo, 0.0, qh)],
                                 axis=0).astype(BF16)
        chunks, m = [], None
        for c in range(blk + 1):
            st = lax.dot_general(kh_ref[c * tq:(c + 1) * tq, :], q_both, nt_dims,
                                 preferred_element_type=F32)
            if c == blk:
                st = jnp.where(causal_t, st, NEG_INF)
            cm = jnp.max(st, axis=0, keepdims=True)
            m = cm if m is None else jnp.maximum(m, cm)
            chunks.append(st)
            yield
        out.append((chunks, m))

    def value_phase(blk, chunks, m):
        acc = None
        for c, st in enumerate(chunks):
            part = jnp.dot(vt_ref[:, c * tq:(c + 1) * tq], jnp.exp2(st - m).astype(BF16),
                           preferred_element_type=F32)
            acc = part if acc is None else acc + part
            yield
        l = acc[ATT_DV:ATT_DV + 1, :]
        a1, a2 = acc[0:ATT_DV, 0:tq], acc[0:ATT_DV, tq:]
        l1, l2 = l[:, 0:tq], l[:, tq:]
        ot = a1 * (1.0 / l1) - a2 * (lam / l2)
        ot = ot * lax.rsqrt(jnp.mean(ot * ot, axis=0, keepdims=True) + EPS)
        o = ot.T * (subln_ref[...] * (1.0 - lam_init))
        o_ref[blk * tq:(blk + 1) * tq, :] = o.astype(o_ref.dtype)

    n_blk = seq // tq
    scored = []
    for _ in score_phase(0, scored):
        pass
    for blk in range(n_blk):
        phases = [value_phase(blk, *scored[blk])]
        if blk + 1 < n_blk:
            phases.append(score_phase(blk + 1, scored))
        while phases:
            for ph in list(phases):
                if next(ph, StopIteration) is StopIteration:
                    phases.remove(ph)


def _attn_kernel(q_ref, k_ref, v_ref, cos_ref, sin_ref, qn_ref, kn_ref, lam_ref,
                 subln_ref, o_ref, kh_ref, vt_ref, *, tq, lam_init):
    for hh in range(kh_ref.shape[0]):
        cols = slice(hh * ATT_DV, (hh + 1) * ATT_DV)
        _attn_head(q_ref.at[:, cols], k_ref.at[:, cols], v_ref.at[:, cols], cos_ref, sin_ref,
                   qn_ref, kn_ref, lam_ref, subln_ref, o_ref.at[:, cols], kh_ref.at[hh],
                   vt_ref.at[hh], tq=tq, lam_init=lam_init)


def _attn(p, cos, sin_signed, qn, kn, lam_params, subln, *, layer, batch, seq, tq,
          lam_init, q_col, k_col, v_col, heads_per_step):
    d = ATT_DV
    hp = heads_per_step
    assert ATT_HEADS % hp == 0 and q_col % hp == 0 and k_col % hp == 0 and v_col % hp == 0
    const2 = lambda b, h: (0, 0)
    vec = _layer_row_spec(layer, d, 2)
    return pl.pallas_call(
        functools.partial(_attn_kernel, tq=tq, lam_init=lam_init),
        grid=(batch, ATT_HEADS // hp),
        in_specs=[
            pl.BlockSpec((seq, hp * d), lambda b, h: (b, q_col // hp + h)),
            pl.BlockSpec((seq, hp * d), lambda b, h: (b, k_col // hp + h)),
            pl.BlockSpec((seq, hp * d), lambda b, h: (b, v_col // hp + h)),
            pl.BlockSpec((seq, d), const2),
            pl.BlockSpec((seq, d), const2),
            vec,
            vec,
            pl.BlockSpec((None, 4, ATT_DH), lambda b, h: (layer, 0, 0)),
            vec,
        ],
        out_specs=pl.BlockSpec((seq, hp * d), lambda b, h: (b, h)),
        out_shape=jax.ShapeDtypeStruct((batch * seq, ATT_HEADS * d), BF16),
        scratch_shapes=[pltpu.VMEM((hp, seq, d), BF16),
                        pltpu.VMEM((hp, d + BF16_SUBLANES, seq), BF16)],
        compiler_params=pltpu.CompilerParams(
            dimension_semantics=("arbitrary", "arbitrary"),
            vmem_limit_bytes=VMEM_LIMIT_BYTES,
        ),
        name="diff_attn",
    )(p, p, p, cos, sin_signed, qn, kn, lam_params, subln)


def _out_proj_kernel(x_ref, rnn_ref, att_ref, w_ref, o_ref, wb_ref):
    @pl.when(pl.program_id(0) == 0)
    def _():
        wb_ref[...] = w_ref[...].astype(BF16)

    c = rnn_ref.shape[1]
    mixed = jnp.dot(rnn_ref[...], wb_ref[:c, :], preferred_element_type=F32)
    mixed = mixed + jnp.dot(att_ref[...], wb_ref[c:, :], preferred_element_type=F32)
    o_ref[...] = x_ref[...] + mixed


def _out_proj(x, rnn, att, w, *, layer, tm):
    m, d = x.shape
    c = rnn.shape[1]
    k = w.shape[1]
    return pl.pallas_call(
        _out_proj_kernel,
        grid=(m // tm,),
        in_specs=[
            pl.BlockSpec((tm, d), lambda i: (i, 0)),
            pl.BlockSpec((tm, c), lambda i: (i, 0)),
            pl.BlockSpec((tm, att.shape[1]), lambda i: (i, 0)),
            pl.BlockSpec((None, k, d), lambda i: (layer, 0, 0),
                         pipeline_mode=pl.Buffered(1)),
        ],
        out_specs=pl.BlockSpec((tm, d), lambda i: (i, 0)),
        out_shape=jax.ShapeDtypeStruct((m, d), F32),
        scratch_shapes=[pltpu.VMEM((k, d), BF16)],
        compiler_params=pltpu.CompilerParams(
            dimension_semantics=("arbitrary",),
            vmem_limit_bytes=VMEM_LIMIT_BYTES,
        ),
        name="out_proj",
    )(x, rnn, att, w)


def _block_diag_slabs(w):
    g, n, _ = w.shape
    per = MXU_DIM // n
    w5 = w.reshape(g // per, per, n, 1, n)
    eye = jnp.eye(per, dtype=bool)[None, :, None, :, None]
    return jnp.where(eye, w5, 0.0).reshape(g // per, MXU_DIM, MXU_DIM)


def _rope_tables(seq):
    inv_freq = ROPE_THETA ** (-jnp.arange(0, ATT_DH, 2, dtype=F32) / ATT_DH)
    ang = jnp.arange(seq, dtype=F32)[:, None] * inv_freq[None, :]
    ang = jnp.concatenate([ang, ang], axis=-1)
    cos, sin = jnp.cos(ang), jnp.sin(ang)
    half = ATT_DH // 2
    sin_signed = jnp.concatenate([-sin[:, :half], sin[:, half:]], axis=-1)
    return jnp.tile(cos, (1, 2)), jnp.tile(sin_signed, (1, 2))


def kernel(x, ffn1_norm, ffn1_w_gate, ffn1_w_up, ffn1_w_down, mix_norm, w_in, conv_w, conv_b, gate_a_w, gate_a_b, gate_x_w, gate_x_b, lru_param, rnn_out_norm, q_norm, k_norm, lam_q1, lam_k1, lam_q2, lam_k2, subln, w_out, ffn2_norm, ffn2_w_gate, ffn2_w_up, ffn2_w_down):
    bsz, seq, d = x.shape
    depth = w_in.shape[0]
    d_rnn = conv_w.shape[-1]
    d_q = ATT_HEADS * 2 * ATT_DH
    assert w_in.shape[-1] == 2 * d_rnn + 2 * d_q + ATT_HEADS * ATT_DV
    assert d_rnn % LANES == 0 and d_q % LANES == 0
    q_col = 2 * d_rnn // ATT_DV
    k_col = q_col + d_q // ATT_DV
    v_col = k_col + d_q // ATT_DV

    cos, sin_signed = _rope_tables(seq)
    xf = x.reshape(bsz * seq, d)
    rows = lambda v: v.reshape(depth, 1, -1)
    ffn1_norm, ffn2_norm, mix_norm = rows(ffn1_norm), rows(ffn2_norm), rows(mix_norm)
    conv_b, lru_param, rnn_out_norm = rows(conv_b), rows(lru_param), rows(rnn_out_norm)
    gate_a_b, gate_x_b, subln = rows(gate_a_b), rows(gate_x_b), rows(subln)
    q_gain = rows(jnp.tile(q_norm, (1, 2)))
    k_gain = rows(jnp.tile(k_norm, (1, 2)))
    lam_params = jnp.stack([lam_q1, lam_k1, lam_q2, lam_k2], axis=1)

    for l in range(depth):
        xf = _ffn(xf, ffn1_norm, ffn1_w_gate, ffn1_w_up, ffn1_w_down,
                  layer=l, tm=1024, tf=512)

        p = _mix_in(xf, mix_norm, w_in, layer=l, tm=1024, tn=1024, row_chunks=4)

        out_rnn = _rnn(
            p, conv_w, conv_b,
            _block_diag_slabs(gate_a_w[l]).astype(BF16), gate_a_b,
            _block_diag_slabs(gate_x_w[l]).astype(BF16), gate_x_b,
            lru_param, rnn_out_norm, layer=l, batch=bsz, seq=seq, ts=512)

        lam_init = 0.8 - 0.6 * math.exp(-0.3 * l)
        out_att = _attn(
            p, cos, sin_signed, q_gain, k_gain, lam_params, subln,
            layer=l, batch=bsz, seq=seq, tq=256, lam_init=lam_init,
            q_col=q_col, k_col=k_col, v_col=v_col, heads_per_step=4)

        xf = _out_proj(xf, out_rnn, out_att, w_out, layer=l, tm=512)

        xf = _ffn(xf, ffn2_norm, ffn2_w_gate, ffn2_w_up, ffn2_w_down,
                  layer=l, tm=1024, tf=512)
    return xf.reshape(bsz, seq, d)
```

```python
import functools
import math

import jax
import jax.numpy as jnp
from jax import lax
from jax.experimental import pallas as pl
from jax.experimental.pallas import tpu as pltpu

RNN_BLOCKS = 16
CONV_W = 4
LRU_C = 8.0
ATT_HEADS = 8
ATT_DH = 64
ATT_DV = 2 * ATT_DH
ROPE_THETA = 10000.0
EPS = 1e-6
NEG_INF = -1e30

LANES = 128
SUBLANES = 8
BF16_SUBLANES = 16
MXU_DIM = 256
VMEM_CAPACITY_BYTES = 64 * 1024 * 1024
VMEM_LIMIT_BYTES = 58 * 1024 * 1024
VMEM_SPILL_HEADROOM_BYTES = 8 * 1024 * 1024

F32 = jnp.float32
BF16 = jnp.bfloat16


def _rms_scale(x):
    return lax.rsqrt(jnp.mean(x * x, axis=-1, keepdims=True) + EPS)


def _layer_row_spec(layer, width, n_grid):
    zeros = (0,) * 2
    if n_grid == 1:
        return pl.BlockSpec((None, 1, width), lambda i: (layer, *zeros))
    if n_grid == 2:
        return pl.BlockSpec((None, 1, width), lambda i, j: (layer, *zeros))
    return pl.BlockSpec((None, 1, width), lambda i, j, k: (layer, *zeros))


def _on_staged_row_tile(x_hbm, xs_ref, sem, first_step, later_step):
    i, j = pl.program_id(0), pl.program_id(1)
    tm = xs_ref.shape[0]

    def x_tile_copy(tile):
        start = pl.multiple_of(tile * tm, tm)
        return pltpu.make_async_copy(x_hbm.at[pl.ds(start, tm), :], xs_ref, sem)

    @pl.when((i == 0) & (j == 0))
    def _():
        x_tile_copy(0).start()

    @pl.when(j == 0)
    def _():
        x_tile_copy(i).wait()
        first_step(xs_ref[...])

    @pl.when((j == 1) & (i + 1 < pl.num_programs(0)))
    def _():
        x_tile_copy(i + 1).start()

    @pl.when(j > 0)
    def _():
        later_step()


def _ffn_kernel(x_hbm, gain_ref, wg_ref, wu_ref, wd_ref, o_ref, xs_ref, h_ref, sem):
    def hidden_chunk():
        h = h_ref[...]
        g = jnp.dot(h, wg_ref[...].astype(BF16), preferred_element_type=F32)
        u = jnp.dot(h, wu_ref[...].astype(BF16), preferred_element_type=F32)
        a = (g * jax.nn.sigmoid(g)) * u * 0.5
        return jnp.dot(a.astype(BF16), wd_ref[...].astype(BF16), preferred_element_type=F32)

    def first_step(x):
        h_ref[...] = (x * _rms_scale(x) * gain_ref[...]).astype(BF16)
        o_ref[...] = x + hidden_chunk()

    def later_step():
        o_ref[...] += hidden_chunk()

    _on_staged_row_tile(x_hbm, xs_ref, sem, first_step, later_step)


def _ffn(x, gain, wg, wu, wd, *, layer, tm, tf):
    m, d = x.shape
    f = wg.shape[2]
    assert f // tf >= 2
    window_bytes = 2 * tm * d * 4 + 2 * 3 * d * tf * 4 + tm * d * 4 + tm * d * 2
    vmem_limit = window_bytes + VMEM_SPILL_HEADROOM_BYTES
    assert vmem_limit <= VMEM_CAPACITY_BYTES
    return pl.pallas_call(
        _ffn_kernel,
        grid=(m // tm, f // tf),
        in_specs=[
            pl.BlockSpec(memory_space=pl.ANY),
            _layer_row_spec(layer, d, 2),
            pl.BlockSpec((None, d, tf), lambda i, j: (layer, 0, j)),
            pl.BlockSpec((None, d, tf), lambda i, j: (layer, 0, j)),
            pl.BlockSpec((None, tf, d), lambda i, j: (layer, j, 0)),
        ],
        out_specs=pl.BlockSpec((tm, d), lambda i, j: (i, 0)),
        out_shape=jax.ShapeDtypeStruct((m, d), F32),
        scratch_shapes=[pltpu.VMEM((tm, d), F32), pltpu.VMEM((tm, d), BF16),
                        pltpu.SemaphoreType.DMA(())],
        compiler_params=pltpu.CompilerParams(
            dimension_semantics=("arbitrary", "arbitrary"),
            vmem_limit_bytes=vmem_limit,
        ),
        name="ffn",
    )(x, gain, wg, wu, wd)


def _mix_in_kernel(x_hbm, gain_ref, w_ref, o_ref, xs_ref, h_ref, wb_ref, sem, *, row_chunks):
    i, j = pl.program_id(0), pl.program_id(1)

    @pl.when(i == 0)
    def _():
        wb_ref[j] = w_ref[...].astype(BF16)

    rc = h_ref.shape[0] // row_chunks

    def column_step(x):
        w = wb_ref[j]
        for r in range(row_chunks):
            rows = slice(r * rc, (r + 1) * rc)
            if x is not None:
                xr = x[rows, :]
                h_ref[rows, :] = (xr * _rms_scale(xr) * gain_ref[...]).astype(BF16)
            o_ref[rows, :] = jnp.dot(h_ref[rows, :], w, preferred_element_type=F32)

    _on_staged_row_tile(x_hbm, xs_ref, sem, column_step, functools.partial(column_step, None))


def _mix_in(x, gain, w, *, layer, tm, tn, row_chunks):
    m, d = x.shape
    n = w.shape[2]
    nj = n // tn
    assert nj >= 2
    window_bytes = tm * d * 4 + tm * d * 2 + d * n * 2 + 2 * d * tn * 4 + 2 * tm * tn * 4
    vmem_limit = window_bytes + VMEM_SPILL_HEADROOM_BYTES
    assert vmem_limit <= VMEM_CAPACITY_BYTES
    return pl.pallas_call(
        functools.partial(_mix_in_kernel, row_chunks=row_chunks),
        grid=(m // tm, nj),
        in_specs=[
            pl.BlockSpec(memory_space=pl.ANY),
            _layer_row_spec(layer, d, 2),
            pl.BlockSpec((None, d, tn),
                         lambda i, j: (layer, 0, jnp.where(i == 0, j, nj - 1))),
        ],
        out_specs=pl.BlockSpec((tm, tn), lambda i, j: (i, j)),
        out_shape=jax.ShapeDtypeStruct((m, n), F32),
        scratch_shapes=[pltpu.VMEM((tm, d), F32), pltpu.VMEM((tm, d), BF16),
                        pltpu.VMEM((nj, d, tn), BF16), pltpu.SemaphoreType.DMA(())],
        compiler_params=pltpu.CompilerParams(
            dimension_semantics=("arbitrary", "arbitrary"),
            vmem_limit_bytes=vmem_limit,
        ),
        name="mix_in",
    )(x, gain, w)


def _rnn_kernel(x_ref, g_ref, cw_ref, cb_ref, wa_ref, ba_ref, wx_ref, bx_ref,
                lru_ref, gain_ref, o_ref, xs_ref, hprev_ref):
    ts, c = x_ref.shape

    @pl.when(pl.program_id(1) == 0)
    def _():
        xs_ref[0:SUBLANES, :] = jnp.zeros((SUBLANES, c), F32)
        hprev_ref[...] = jnp.zeros_like(hprev_ref)

    x = x_ref[...]
    xs_ref[SUBLANES:, :] = x
    xc = cb_ref[...] + x * cw_ref[CONV_W - 1:CONV_W, :]
    for k in range(CONV_W - 1):
        start = SUBLANES - (CONV_W - 1 - k)
        xc = xc + xs_ref[start:start + ts, :] * cw_ref[k:k + 1, :]
    xs_ref[0:SUBLANES, :] = x[ts - SUBLANES:, :]

    xcb = xc.astype(BF16)
    n_slabs = c // MXU_DIM

    def gate(w_ref, b_ref):
        parts = [
            jnp.dot(xcb[:, s * MXU_DIM:(s + 1) * MXU_DIM], w_ref[s],
                    preferred_element_type=F32)
            for s in range(n_slabs)
        ]
        return jax.nn.sigmoid(jnp.concatenate(parts, axis=1) + b_ref[...])

    r = gate(wa_ref, ba_ref)
    i = gate(wx_ref, bx_ref)

    neg_l = -lru_ref[...]
    softplus = jnp.maximum(neg_l, 0.0) + jnp.log1p(jnp.exp(-jnp.abs(neg_l)))
    log_a = (-LRU_C * r) * softplus
    a = jnp.exp(log_a)
    mult = jnp.sqrt(-jnp.tanh(log_a) * (1.0 + a * a))
    u = mult * (i * xc)

    sub = lax.broadcasted_iota(jnp.int32, (SUBLANES, c), 0)
    carry = hprev_ref[0:1, :]
    groups = []
    for g in range(ts // SUBLANES):
        rows = slice(g * SUBLANES, (g + 1) * SUBLANES)
        acc_a, acc_h = a[rows, :], u[rows, :]
        k = 1
        while k < SUBLANES:
            keep = sub >= k
            sh_a = pltpu.roll(acc_a, k, axis=0)
            sh_h = pltpu.roll(acc_h, k, axis=0)
            acc_h = jnp.where(keep, acc_a * sh_h + acc_h, acc_h)
            acc_a = jnp.where(keep, acc_a * sh_a, acc_a)
            k *= 2
        h_g = acc_h + acc_a * carry
        carry = h_g[SUBLANES - 1:SUBLANES, :]
        groups.append(h_g)
    h = jnp.concatenate(groups, axis=0)
    hprev_ref[0:1, :] = carry

    y = jax.nn.gelu(g_ref[...]) * h
    o_ref[...] = (y * _rms_scale(y) * gain_ref[...]).astype(o_ref.dtype)


def _qk_prep(t, gain, cos, sin_signed):
    lane = lax.broadcasted_iota(jnp.int32, t.shape, 1)
    lo = lane < ATT_DH
    t2 = t * t
    ss_lo = jnp.sum(jnp.where(lo, t2, 0.0), axis=-1, keepdims=True)
    ss_hi = jnp.sum(jnp.where(lo, 0.0, t2), axis=-1, keepdims=True)
    ms = jnp.where(lo, ss_lo, ss_hi) * (1.0 / ATT_DH)
    y = t * lax.rsqrt(ms + EPS) * gain
    half = ATT_DH // 2
    ahead = pltpu.roll(y, LANES - half, axis=1)
    behind = pltpu.roll(y, half, axis=1)
    rot = jnp.where((lane & half) == 0, ahead, behind)
    return y * cos + rot * sin_signed


def _attn_head(q_ref, k_ref, v_ref, cos_ref, sin_ref, qn_ref, kn_ref, lam_ref,
               subln_ref, o_ref, kh_ref, vt_ref, *, tq, lam_init):
    seq = q_ref.shape[0]
    kh_ref[...] = _qk_prep(k_ref[...].astype(F32), kn_ref[...], cos_ref[...],
                           sin_ref[...]).astype(BF16)
    vt_ref[0:ATT_DV, :] = v_ref[...].astype(F32).T.astype(BF16)
    vt_ref[ATT_DV:, :] = jnp.ones((vt_ref.shape[0] - ATT_DV, seq), BF16)

    lp = lam_ref[...]
    lam = (jnp.exp(jnp.sum(lp[0:1] * lp[1:2], axis=-1, keepdims=True))
           - jnp.exp(jnp.sum(lp[2:3] * lp[3:4], axis=-1, keepdims=True))
           + lam_init)

    lane = lax.broadcasted_iota(jnp.int32, (tq, ATT_DV), 1)
    lo = lane < ATT_DH
    causal_t = (lax.broadcasted_iota(jnp.int32, (tq, 2 * tq), 0)
                <= (lax.broadcasted_iota(jnp.int32, (tq, 2 * tq), 1) & (tq - 1)))
    nt_dims = (((1,), (1,)), ((), ()))

    def score_phase(blk, out):
        rows = slice(blk * tq, (blk + 1) * tq)
        qh = _qk_prep(q_ref[rows, :].astype(F32), qn_ref[...], cos_ref[rows, :],
                      sin_ref[rows, :])
        qh = qh * (math.log2(math.e) / math.sqrt(ATT_DH))
        q_both = jnp.concatenate([jnp.where(lo, qh, 0.0), jnp.where(lo, 0.0, qh)],
                                 axis=0).astype(BF16)
        chunks, m = [], None
        for c in range(blk + 1):
            st = lax.dot_general(kh_ref[c * tq:(c + 1) * tq, :], q_both, nt_dims,
                                 preferred_element_type=F32)
            if c == blk:
                st = jnp.where(causal_t, st, NEG_INF)
            cm = jnp.max(st, axis=0, keepdims=True)
            m = cm if m is None else jnp.maximum(m, cm)
            chunks.append(st)
            yield
        out.append((chunks, m))

    def value_phase(blk, chunks, m):
        acc = None
        for c, st in enumerate(chunks):
            part = jnp.dot(vt_ref[:, c * tq:(c + 1) * tq], jnp.exp2(st - m).astype(BF16),
                           preferred_element_type=F32)
            acc = part if acc is None else acc + part
            yield
        l = acc[ATT_DV:ATT_DV + 1, :]
        a1, a2 = acc[0:ATT_DV, 0:tq], acc[0:ATT_DV, tq:]
        l1, l2 = l[:, 0:tq], l[:, tq:]
        ot = a1 * (1.0 / l1) - a2 * (lam / l2)
        ot = ot * lax.rsqrt(jnp.mean(ot * ot, axis=0, keepdims=True) + EPS)
        o = ot.T * (subln_ref[...] * (1.0 - lam_init))
        o_ref[blk * tq:(blk + 1) * tq, :] = o.astype(o_ref.dtype)

    n_blk = seq // tq
    scored = []
    for _ in score_phase(0, scored):
        pass
    for blk in range(n_blk):
        phases = [value_phase(blk, *scored[blk])]
        if blk + 1 < n_blk:
            phases.append(score_phase(blk + 1, scored))
        while phases:
            for ph in list(phases):
                if next(ph, StopIteration) is StopIteration:
                    phases.remove(ph)


def _attn_kernel(q_ref, k_ref, v_ref, cos_ref, sin_ref, qn_ref, kn_ref, lam_ref,
                 subln_ref, o_ref, kh_ref, vt_ref, *, tq, lam_init):
    for hh in range(kh_ref.shape[0]):
        cols = slice(hh * ATT_DV, (hh + 1) * ATT_DV)
        _attn_head(q_ref.at[:, cols], k_ref.at[:, cols], v_ref.at[:, cols], cos_ref, sin_ref,
                   qn_ref, kn_ref, lam_ref, subln_ref, o_ref.at[:, cols], kh_ref.at[hh],
                   vt_ref.at[hh], tq=tq, lam_init=lam_init)


N_RNN_IN = 10


def _mixers_kernel(*refs, tq, lam_init):
    rnn_in, att_in = refs[:N_RNN_IN], refs[N_RNN_IN:-6]
    rnn_out, att_out, xs_ref, hprev_ref, kh_ref, vt_ref = refs[-6:]
    _rnn_kernel(*rnn_in, rnn_out, xs_ref, hprev_ref)
    _attn_kernel(*att_in, att_out, kh_ref, vt_ref, tq=tq, lam_init=lam_init)


def _mixers(p, conv_w, conv_b, wa_bd, ba, wx_bd, bx, lru, rnn_gain, cos, sin_signed, qn, kn,
            lam_params, subln, *, layer, batch, seq, ts, tq, lam_init, q_col, k_col, v_col):
    c = conv_w.shape[2]
    d = ATT_DV
    nt = seq // ts
    assert ATT_HEADS % nt == 0
    hp = ATT_HEADS // nt
    assert q_col % hp == 0 and k_col % hp == 0 and v_col % hp == 0
    row = lambda b, s: (b * nt + s, 0)
    gate_row = lambda b, s: (b * nt + s, 1)
    const2 = lambda b, s: (0, 0)
    const3 = lambda b, s: (0, 0, 0)
    vec_c = _layer_row_spec(layer, c, 2)
    vec_d = _layer_row_spec(layer, d, 2)
    rnn_specs = [
        pl.BlockSpec((ts, c), row),
        pl.BlockSpec((ts, c), gate_row),
        pl.BlockSpec((None, CONV_W, c), lambda b, s: (layer, 0, 0)),
        vec_c,
        pl.BlockSpec(wa_bd.shape, const3),
        vec_c,
        pl.BlockSpec(wx_bd.shape, const3),
        vec_c,
        vec_c,
        vec_c,
    ]
    assert len(rnn_specs) == N_RNN_IN
    att_specs = [
        pl.BlockSpec((seq, hp * d), lambda b, s: (b, q_col // hp + s)),
        pl.BlockSpec((seq, hp * d), lambda b, s: (b, k_col // hp + s)),
        pl.BlockSpec((seq, hp * d), lambda b, s: (b, v_col // hp + s)),
        pl.BlockSpec((seq, d), const2),
        pl.BlockSpec((seq, d), const2),
        vec_d,
        vec_d,
        pl.BlockSpec((None, 4, ATT_DH), lambda b, s: (layer, 0, 0)),
        vec_d,
    ]
    return pl.pallas_call(
        functools.partial(_mixers_kernel, tq=tq, lam_init=lam_init),
        grid=(batch, nt),
        in_specs=rnn_specs + att_specs,
        out_specs=[pl.BlockSpec((ts, c), row),
                   pl.BlockSpec((seq, hp * d), lambda b, s: (b, s))],
        out_shape=[jax.ShapeDtypeStruct((batch * seq, c), BF16),
                   jax.ShapeDtypeStruct((batch * seq, ATT_HEADS * d), BF16)],
        scratch_shapes=[pltpu.VMEM((SUBLANES + ts, c), F32), pltpu.VMEM((SUBLANES, c), F32),
                        pltpu.VMEM((hp, seq, d), BF16),
                        pltpu.VMEM((hp, d + BF16_SUBLANES, seq), BF16)],
        compiler_params=pltpu.CompilerParams(
            dimension_semantics=("arbitrary", "arbitrary"),
            vmem_limit_bytes=VMEM_LIMIT_BYTES,
        ),
        name="mixers",
    )(p, p, conv_w, conv_b, wa_bd, ba, wx_bd, bx, lru, rnn_gain,
      p, p, p, cos, sin_signed, qn, kn, lam_params, subln)


def _out_proj_kernel(x_ref, rnn_ref, att_ref, w_ref, o_ref, wb_ref):
    @pl.when(pl.program_id(0) == 0)
    def _():
        wb_ref[...] = w_ref[...].astype(BF16)

    c = rnn_ref.shape[1]
    mixed = jnp.dot(rnn_ref[...], wb_ref[:c, :], preferred_element_type=F32)
    mixed = mixed + jnp.dot(att_ref[...], wb_ref[c:, :], preferred_element_type=F32)
    o_ref[...] = x_ref[...] + mixed


def _out_proj(x, rnn, att, w, *, layer, tm):
    m, d = x.shape
    c = rnn.shape[1]
    k = w.shape[1]
    return pl.pallas_call(
        _out_proj_kernel,
        grid=(m // tm,),
        in_specs=[
            pl.BlockSpec((tm, d), lambda i: (i, 0)),
            pl.BlockSpec((tm, c), lambda i: (i, 0)),
            pl.BlockSpec((tm, att.shape[1]), lambda i: (i, 0)),
            pl.BlockSpec((None, k, d), lambda i: (layer, 0, 0),
                         pipeline_mode=pl.Buffered(1)),
        ],
        out_specs=pl.BlockSpec((tm, d), lambda i: (i, 0)),
        out_shape=jax.ShapeDtypeStruct((m, d), F32),
        scratch_shapes=[pltpu.VMEM((k, d), BF16)],
        compiler_params=pltpu.CompilerParams(
            dimension_semantics=("arbitrary",),
            vmem_limit_bytes=VMEM_LIMIT_BYTES,
        ),
        name="out_proj",
    )(x, rnn, att, w)


def _block_diag_slabs(w):
    g, n, _ = w.shape
    per = MXU_DIM // n
    w5 = w.reshape(g // per, per, n, 1, n)
    eye = jnp.eye(per, dtype=bool)[None, :, None, :, None]
    return jnp.where(eye, w5, 0.0).reshape(g // per, MXU_DIM, MXU_DIM)


def _rope_tables(seq):
    inv_freq = ROPE_THETA ** (-jnp.arange(0, ATT_DH, 2, dtype=F32) / ATT_DH)
    ang = jnp.arange(seq, dtype=F32)[:, None] * inv_freq[None, :]
    ang = jnp.concatenate([ang, ang], axis=-1)
    cos, sin = jnp.cos(ang), jnp.sin(ang)
    half = ATT_DH // 2
    sin_signed = jnp.concatenate([-sin[:, :half], sin[:, half:]], axis=-1)
    return jnp.tile(cos, (1, 2)), jnp.tile(sin_signed, (1, 2))


def kernel(x, ffn1_norm, ffn1_w_gate, ffn1_w_up, ffn1_w_down, mix_norm, w_in, conv_w, conv_b, gate_a_w, gate_a_b, gate_x_w, gate_x_b, lru_param, rnn_out_norm, q_norm, k_norm, lam_q1, lam_k1, lam_q2, lam_k2, subln, w_out, ffn2_norm, ffn2_w_gate, ffn2_w_up, ffn2_w_down):
    bsz, seq, d = x.shape
    depth = w_in.shape[0]
    d_rnn = conv_w.shape[-1]
    d_q = ATT_HEADS * 2 * ATT_DH
    assert w_in.shape[-1] == 2 * d_rnn + 2 * d_q + ATT_HEADS * ATT_DV
    assert d_rnn % LANES == 0 and d_q % LANES == 0
    q_col = 2 * d_rnn // ATT_DV
    k_col = q_col + d_q // ATT_DV
    v_col = k_col + d_q // ATT_DV

    cos, sin_signed = _rope_tables(seq)
    xf = x.reshape(bsz * seq, d)
    rows = lambda v: v.reshape(depth, 1, -1)
    ffn1_norm, ffn2_norm, mix_norm = rows(ffn1_norm), rows(ffn2_norm), rows(mix_norm)
    conv_b, lru_param, rnn_out_norm = rows(conv_b), rows(lru_param), rows(rnn_out_norm)
    gate_a_b, gate_x_b, subln = rows(gate_a_b), rows(gate_x_b), rows(subln)
    q_gain = rows(jnp.tile(q_norm, (1, 2)))
    k_gain = rows(jnp.tile(k_norm, (1, 2)))
    lam_params = jnp.stack([lam_q1, lam_k1, lam_q2, lam_k2], axis=1)

    for l in range(depth):
        xf = _ffn(xf, ffn1_norm, ffn1_w_gate, ffn1_w_up, ffn1_w_down,
                  layer=l, tm=1024, tf=512)

        p = _mix_in(xf, mix_norm, w_in, layer=l, tm=1024, tn=1024, row_chunks=4)

        lam_init = 0.8 - 0.6 * math.exp(-0.3 * l)
        out_rnn, out_att = _mixers(
            p, conv_w, conv_b,
            _block_diag_slabs(gate_a_w[l]).astype(BF16), gate_a_b,
            _block_diag_slabs(gate_x_w[l]).astype(BF16), gate_x_b,
            lru_param, rnn_out_norm, cos, sin_signed, q_gain, k_gain, lam_params, subln,
            layer=l, batch=bsz, seq=seq, ts=512, tq=256, lam_init=lam_init,
            q_col=q_col, k_col=k_col, v_col=v_col)

        xf = _out_proj(xf, out_rnn, out_att, w_out, layer=l, tm=512)

        xf = _ffn(xf, ffn2_norm, ffn2_w_gate, ffn2_w_up, ffn2_w_down,
                  layer=l, tm=1024, tf=512)
    return xf.reshape(bsz, seq, d)
```

```python
import functools
import math

import jax
import jax.numpy as jnp
from jax import lax
from jax.experimental import pallas as pl
from jax.experimental.pallas import tpu as pltpu

CONV_W = 4
LRU_C = 8.0
ATT_HEADS = 8
ATT_DH = 64
ATT_DV = 2 * ATT_DH
ROPE_THETA = 10000.0
EPS = 1e-6
NEG_INF = -1e30

LANES = 128
SUBLANES = 8
BF16_SUBLANES = 16
MXU_DIM = 256
VMEM_CAPACITY_BYTES = 64 * 1024 * 1024
VMEM_LIMIT_BYTES = 58 * 1024 * 1024
VMEM_SPILL_HEADROOM_BYTES = 8 * 1024 * 1024

F32 = jnp.float32
BF16 = jnp.bfloat16


def _rms_scale(x):
    return lax.rsqrt(jnp.mean(x * x, axis=-1, keepdims=True) + EPS)


def _layer_row_spec(layer, width):
    return pl.BlockSpec((None, 1, width), lambda i, j: (layer, 0, 0))


def _on_staged_row_tile(x_hbm, xs_ref, sem, first_step, later_step):
    i, j = pl.program_id(0), pl.program_id(1)
    tm = xs_ref.shape[0]

    def x_tile_copy(tile):
        start = pl.multiple_of(tile * tm, tm)
        return pltpu.make_async_copy(x_hbm.at[pl.ds(start, tm), :], xs_ref, sem)

    @pl.when((i == 0) & (j == 0))
    def _():
        x_tile_copy(0).start()

    @pl.when(j == 0)
    def _():
        x_tile_copy(i).wait()
        first_step(xs_ref[...])

    @pl.when((j == 1) & (i + 1 < pl.num_programs(0)))
    def _():
        x_tile_copy(i + 1).start()

    @pl.when(j > 0)
    def _():
        later_step()


def _ffn_kernel(x_hbm, gain_ref, wg_ref, wu_ref, wd_ref, o_ref, xs_ref, h_ref, sem):
    def hidden_chunk():
        h = h_ref[...]
        g = jnp.dot(h, wg_ref[...].astype(BF16), preferred_element_type=F32)
        u = jnp.dot(h, wu_ref[...].astype(BF16), preferred_element_type=F32)
        a = (g * jax.nn.sigmoid(g)) * u * 0.5
        return jnp.dot(a.astype(BF16), wd_ref[...].astype(BF16), preferred_element_type=F32)

    def first_step(x):
        h_ref[...] = (x * _rms_scale(x) * gain_ref[...]).astype(BF16)
        o_ref[...] = x + hidden_chunk()

    def later_step():
        o_ref[...] += hidden_chunk()

    _on_staged_row_tile(x_hbm, xs_ref, sem, first_step, later_step)


def _ffn(x, gain, wg, wu, wd, *, layer, tm, tf):
    m, d = x.shape
    f = wg.shape[2]
    assert f // tf >= 2
    window_bytes = 2 * tm * d * 4 + 2 * 3 * d * tf * 4 + tm * d * 4 + tm * d * 2
    vmem_limit = window_bytes + VMEM_SPILL_HEADROOM_BYTES
    assert vmem_limit <= VMEM_CAPACITY_BYTES
    return pl.pallas_call(
        _ffn_kernel,
        grid=(m // tm, f // tf),
        in_specs=[
            pl.BlockSpec(memory_space=pl.ANY),
            _layer_row_spec(layer, d),
            pl.BlockSpec((None, d, tf), lambda i, j: (layer, 0, j)),
            pl.BlockSpec((None, d, tf), lambda i, j: (layer, 0, j)),
            pl.BlockSpec((None, tf, d), lambda i, j: (layer, j, 0)),
        ],
        out_specs=pl.BlockSpec((tm, d), lambda i, j: (i, 0)),
        out_shape=jax.ShapeDtypeStruct((m, d), F32),
        scratch_shapes=[pltpu.VMEM((tm, d), F32), pltpu.VMEM((tm, d), BF16),
                        pltpu.SemaphoreType.DMA(())],
        compiler_params=pltpu.CompilerParams(
            dimension_semantics=("arbitrary", "arbitrary"),
            vmem_limit_bytes=vmem_limit,
        ),
        name="ffn",
    )(x, gain, wg, wu, wd)


def _mix_in_kernel(x_hbm, gain_ref, w_ref, o_ref, xs_ref, h_ref, wb_ref, sem, *, row_chunks):
    i, j = pl.program_id(0), pl.program_id(1)

    @pl.when(i == 0)
    def _():
        wb_ref[j] = w_ref[...].astype(BF16)

    rc = h_ref.shape[0] // row_chunks

    def column_step(x):
        w = wb_ref[j]
        for r in range(row_chunks):
            rows = slice(r * rc, (r + 1) * rc)
            if x is not None:
                xr = x[rows, :]
                h_ref[rows, :] = (xr * _rms_scale(xr) * gain_ref[...]).astype(BF16)
            o_ref[rows, :] = jnp.dot(h_ref[rows, :], w, preferred_element_type=F32)

    _on_staged_row_tile(x_hbm, xs_ref, sem, column_step, functools.partial(column_step, None))


def _mix_in(x, gain, w, *, layer, tm, tn, row_chunks):
    m, d = x.shape
    n = w.shape[2]
    nj = n // tn
    assert nj >= 2
    window_bytes = tm * d * 4 + tm * d * 2 + d * n * 2 + 2 * d * tn * 4 + 2 * tm * tn * 4
    vmem_limit = window_bytes + VMEM_SPILL_HEADROOM_BYTES
    assert vmem_limit <= VMEM_CAPACITY_BYTES
    return pl.pallas_call(
        functools.partial(_mix_in_kernel, row_chunks=row_chunks),
        grid=(m // tm, nj),
        in_specs=[
            pl.BlockSpec(memory_space=pl.ANY),
            _layer_row_spec(layer, d),
            pl.BlockSpec((None, d, tn),
                         lambda i, j: (layer, 0, jnp.where(i == 0, j, nj - 1))),
        ],
        out_specs=pl.BlockSpec((tm, tn), lambda i, j: (i, j)),
        out_shape=jax.ShapeDtypeStruct((m, n), F32),
        scratch_shapes=[pltpu.VMEM((tm, d), F32), pltpu.VMEM((tm, d), BF16),
                        pltpu.VMEM((nj, d, tn), BF16), pltpu.SemaphoreType.DMA(())],
        compiler_params=pltpu.CompilerParams(
            dimension_semantics=("arbitrary", "arbitrary"),
            vmem_limit_bytes=vmem_limit,
        ),
        name="mix_in",
    )(x, gain, w)


def _rnn_kernel(x_ref, g_ref, cw_ref, cb_ref, wa_ref, ba_ref, wx_ref, bx_ref,
                lru_ref, gain_ref, o_ref, xs_ref, hprev_ref):
    ts, c = x_ref.shape

    @pl.when(pl.program_id(1) == 0)
    def _():
        xs_ref[0:SUBLANES, :] = jnp.zeros((SUBLANES, c), F32)
        hprev_ref[...] = jnp.zeros_like(hprev_ref)

    x = x_ref[...]
    xs_ref[SUBLANES:, :] = x
    xc = cb_ref[...] + x * cw_ref[CONV_W - 1:CONV_W, :]
    for k in range(CONV_W - 1):
        start = SUBLANES - (CONV_W - 1 - k)
        xc = xc + xs_ref[start:start + ts, :] * cw_ref[k:k + 1, :]
    xs_ref[0:SUBLANES, :] = x[ts - SUBLANES:, :]

    xcb = xc.astype(BF16)
    n_slabs = c // MXU_DIM

    def gate(w_ref, b_ref):
        parts = [
            jnp.dot(xcb[:, s * MXU_DIM:(s + 1) * MXU_DIM], w_ref[s],
                    preferred_element_type=F32)
            for s in range(n_slabs)
        ]
        return jax.nn.sigmoid(jnp.concatenate(parts, axis=1) + b_ref[...])

    r = gate(wa_ref, ba_ref)
    i = gate(wx_ref, bx_ref)

    neg_l = -lru_ref[...]
    softplus = jnp.maximum(neg_l, 0.0) + jnp.log1p(jnp.exp(-jnp.abs(neg_l)))
    log_a = (-LRU_C * r) * softplus
    a = jnp.exp(log_a)
    mult = jnp.sqrt(-jnp.tanh(log_a) * (1.0 + a * a))
    u = mult * (i * xc)

    sub = lax.broadcasted_iota(jnp.int32, (SUBLANES, c), 0)
    carry = hprev_ref[0:1, :]
    groups = []
    for g in range(ts // SUBLANES):
        rows = slice(g * SUBLANES, (g + 1) * SUBLANES)
        acc_a, acc_h = a[rows, :], u[rows, :]
        k = 1
        while k < SUBLANES:
            keep = sub >= k
            sh_a = pltpu.roll(acc_a, k, axis=0)
            sh_h = pltpu.roll(acc_h, k, axis=0)
            acc_h = jnp.where(keep, acc_a * sh_h + acc_h, acc_h)
            acc_a = jnp.where(keep, acc_a * sh_a, acc_a)
            k *= 2
        h_g = acc_h + acc_a * carry
        carry = h_g[SUBLANES - 1:SUBLANES, :]
        groups.append(h_g)
    h = jnp.concatenate(groups, axis=0)
    hprev_ref[0:1, :] = carry

    y = jax.nn.gelu(g_ref[...]) * h
    o_ref[...] = (y * _rms_scale(y) * gain_ref[...]).astype(o_ref.dtype)


def _qk_prep(t, gain, cos, sin_signed):
    lane = lax.broadcasted_iota(jnp.int32, t.shape, 1)
    lo = lane < ATT_DH
    t2 = t * t
    ss_lo = jnp.sum(jnp.where(lo, t2, 0.0), axis=-1, keepdims=True)
    ss_hi = jnp.sum(jnp.where(lo, 0.0, t2), axis=-1, keepdims=True)
    ms = jnp.where(lo, ss_lo, ss_hi) * (1.0 / ATT_DH)
    y = t * lax.rsqrt(ms + EPS) * gain
    half = ATT_DH // 2
    ahead = pltpu.roll(y, LANES - half, axis=1)
    behind = pltpu.roll(y, half, axis=1)
    rot = jnp.where((lane & half) == 0, ahead, behind)
    return y * cos + rot * sin_signed


def _attn_head(q_ref, k_ref, v_ref, cos_ref, sin_ref, qn_ref, kn_ref, lam_ref,
               subln_ref, o_ref, kh_ref, vt_ref, *, tq, lam_init):
    seq = q_ref.shape[0]
    kh_ref[...] = _qk_prep(k_ref[...].astype(F32), kn_ref[...], cos_ref[...],
                           sin_ref[...]).astype(BF16)
    vt_ref[0:ATT_DV, :] = v_ref[...].astype(F32).T.astype(BF16)
    vt_ref[ATT_DV:, :] = jnp.ones((vt_ref.shape[0] - ATT_DV, seq), BF16)

    lp = lam_ref[...]
    lam = (jnp.exp(jnp.sum(lp[0:1] * lp[1:2], axis=-1, keepdims=True))
           - jnp.exp(jnp.sum(lp[2:3] * lp[3:4], axis=-1, keepdims=True))
           + lam_init)

    lane = lax.broadcasted_iota(jnp.int32, (tq, ATT_DV), 1)
    lo = lane < ATT_DH
    causal_t = (lax.broadcasted_iota(jnp.int32, (tq, 2 * tq), 0)
                <= (lax.broadcasted_iota(jnp.int32, (tq, 2 * tq), 1) & (tq - 1)))
    nt_dims = (((1,), (1,)), ((), ()))

    def score_phase(blk, out):
        rows = slice(blk * tq, (blk + 1) * tq)
        qh = _qk_prep(q_ref[rows, :].astype(F32), qn_ref[...], cos_ref[rows, :],
                      sin_ref[rows, :])
        qh = qh * (math.log2(math.e) / math.sqrt(ATT_DH))
        q_both = jnp.concatenate([jnp.where(lo, qh, 0.0), jnp.where(lo, 0.0, qh)],
                                 axis=0).astype(BF16)
        chunks, m = [], None
        for c in range(blk + 1):
            st = lax.dot_general(kh_ref[c * tq:(c + 1) * tq, :], q_both, nt_dims,
                                 preferred_element_type=F32)
            if c == blk:
                st = jnp.where(causal_t, st, NEG_INF)
            cm = jnp.max(st, axis=0, keepdims=True)
            m = cm if m is None else jnp.maximum(m, cm)
            chunks.append(st)
            yield
        out.append((chunks, m))

    def value_phase(blk, chunks, m):
        acc = None
        for c, st in enumerate(chunks):
            part = jnp.dot(vt_ref[:, c * tq:(c + 1) * tq], jnp.exp2(st - m).astype(BF16),
                           preferred_element_type=F32)
            acc = part if acc is None else acc + part
            yield
        l = acc[ATT_DV:ATT_DV + 1, :]
        a1, a2 = acc[0:ATT_DV, 0:tq], acc[0:ATT_DV, tq:]
        l1, l2 = l[:, 0:tq], l[:, tq:]
        ot = a1 * (1.0 / l1) - a2 * (lam / l2)
        ot = ot * lax.rsqrt(jnp.mean(ot * ot, axis=0, keepdims=True) + EPS)
        o = ot.T * (subln_ref[...] * (1.0 - lam_init))
        o_ref[blk * tq:(blk + 1) * tq, :] = o.astype(o_ref.dtype)

    n_blk = seq // tq
    scored = []
    for _ in score_phase(0, scored):
        pass
    for blk in range(n_blk):
        phases = [value_phase(blk, *scored[blk])]
        if blk + 1 < n_blk:
            phases.append(score_phase(blk + 1, scored))
        while phases:
            for ph in list(phases):
                if next(ph, StopIteration) is StopIteration:
                    phases.remove(ph)


def _attn_kernel(q_ref, k_ref, v_ref, cos_ref, sin_ref, qn_ref, kn_ref, lam_ref,
                 subln_ref, o_ref, kh_ref, vt_ref, *, tq, lam_init):
    for hh in range(kh_ref.shape[0]):
        cols = slice(hh * ATT_DV, (hh + 1) * ATT_DV)
        _attn_head(q_ref.at[:, cols], k_ref.at[:, cols], v_ref.at[:, cols], cos_ref, sin_ref,
                   qn_ref, kn_ref, lam_ref, subln_ref, o_ref.at[:, cols], kh_ref.at[hh],
                   vt_ref.at[hh], tq=tq, lam_init=lam_init)


N_RNN_IN = 10


def _mixers_kernel(*refs, tq, lam_init):
    rnn_in, att_in = refs[:N_RNN_IN], refs[N_RNN_IN:-6]
    rnn_out, att_out, xs_ref, hprev_ref, kh_ref, vt_ref = refs[-6:]
    _rnn_kernel(*rnn_in, rnn_out, xs_ref, hprev_ref)
    _attn_kernel(*att_in, att_out, kh_ref, vt_ref, tq=tq, lam_init=lam_init)


def _mixers(p, conv_w, conv_b, wa_bd, ba, wx_bd, bx, lru, rnn_gain, cos, sin_signed, qn, kn,
            lam_params, subln, *, layer, batch, seq, ts, tq, lam_init, q_col, k_col, v_col):
    c = conv_w.shape[2]
    d = ATT_DV
    nt = seq // ts
    assert ATT_HEADS % nt == 0
    hp = ATT_HEADS // nt
    assert q_col % hp == 0 and k_col % hp == 0 and v_col % hp == 0
    row = lambda b, s: (b * nt + s, 0)
    gate_row = lambda b, s: (b * nt + s, 1)
    const2 = lambda b, s: (0, 0)
    const3 = lambda b, s: (0, 0, 0)
    vec_c = _layer_row_spec(layer, c)
    vec_d = _layer_row_spec(layer, d)
    rnn_specs = [
        pl.BlockSpec((ts, c), row),
        pl.BlockSpec((ts, c), gate_row),
        pl.BlockSpec((None, CONV_W, c), lambda b, s: (layer, 0, 0)),
        vec_c,
        pl.BlockSpec(wa_bd.shape, const3),
        vec_c,
        pl.BlockSpec(wx_bd.shape, const3),
        vec_c,
        vec_c,
        vec_c,
    ]
    assert len(rnn_specs) == N_RNN_IN
    att_specs = [
        pl.BlockSpec((seq, hp * d), lambda b, s: (b, q_col // hp + s)),
        pl.BlockSpec((seq, hp * d), lambda b, s: (b, k_col // hp + s)),
        pl.BlockSpec((seq, hp * d), lambda b, s: (b, v_col // hp + s)),
        pl.BlockSpec((seq, d), const2),
        pl.BlockSpec((seq, d), const2),
        vec_d,
        vec_d,
        pl.BlockSpec((None, 4, ATT_DH), lambda b, s: (layer, 0, 0)),
        vec_d,
    ]
    return pl.pallas_call(
        functools.partial(_mixers_kernel, tq=tq, lam_init=lam_init),
        grid=(batch, nt),
        in_specs=rnn_specs + att_specs,
        out_specs=[pl.BlockSpec((ts, c), row),
                   pl.BlockSpec((seq, hp * d), lambda b, s: (b, s))],
        out_shape=[jax.ShapeDtypeStruct((batch * seq, c), BF16),
                   jax.ShapeDtypeStruct((batch * seq, ATT_HEADS * d), BF16)],
        scratch_shapes=[pltpu.VMEM((SUBLANES + ts, c), F32), pltpu.VMEM((SUBLANES, c), F32),
                        pltpu.VMEM((hp, seq, d), BF16),
                        pltpu.VMEM((hp, d + BF16_SUBLANES, seq), BF16)],
        compiler_params=pltpu.CompilerParams(
            dimension_semantics=("arbitrary", "arbitrary"),
            vmem_limit_bytes=VMEM_LIMIT_BYTES,
        ),
        name="mixers",
    )(p, p, conv_w, conv_b, wa_bd, ba, wx_bd, bx, lru, rnn_gain,
      p, p, p, cos, sin_signed, qn, kn, lam_params, subln)


def _out_proj_kernel(x_ref, rnn_ref, att_ref, w_ref, o_ref, wb_ref):
    @pl.when(pl.program_id(0) == 0)
    def _():
        wb_ref[...] = w_ref[...].astype(BF16)

    c = rnn_ref.shape[1]
    mixed = jnp.dot(rnn_ref[...], wb_ref[:c, :], preferred_element_type=F32)
    mixed = mixed + jnp.dot(att_ref[...], wb_ref[c:, :], preferred_element_type=F32)
    o_ref[...] = x_ref[...] + mixed


def _out_proj(x, rnn, att, w, *, layer, tm):
    m, d = x.shape
    c = rnn.shape[1]
    k = w.shape[1]
    return pl.pallas_call(
        _out_proj_kernel,
        grid=(m // tm,),
        in_specs=[
            pl.BlockSpec((tm, d), lambda i: (i, 0)),
            pl.BlockSpec((tm, c), lambda i: (i, 0)),
            pl.BlockSpec((tm, att.shape[1]), lambda i: (i, 0)),
            pl.BlockSpec((None, k, d), lambda i: (layer, 0, 0),
                         pipeline_mode=pl.Buffered(1)),
        ],
        out_specs=pl.BlockSpec((tm, d), lambda i: (i, 0)),
        out_shape=jax.ShapeDtypeStruct((m, d), F32),
        scratch_shapes=[pltpu.VMEM((k, d), BF16)],
        compiler_params=pltpu.CompilerParams(
            dimension_semantics=("arbitrary",),
            vmem_limit_bytes=VMEM_LIMIT_BYTES,
        ),
        name="out_proj",
    )(x, rnn, att, w)


def _block_diag_slabs(w):
    g, n, _ = w.shape
    per = MXU_DIM // n
    w5 = w.reshape(g // per, per, n, 1, n)
    eye = jnp.eye(per, dtype=bool)[None, :, None, :, None]
    return jnp.where(eye, w5, 0.0).reshape(g // per, MXU_DIM, MXU_DIM)


def _rope_tables(seq):
    inv_freq = ROPE_THETA ** (-jnp.arange(0, ATT_DH, 2, dtype=F32) / ATT_DH)
    ang = jnp.arange(seq, dtype=F32)[:, None] * inv_freq[None, :]
    ang = jnp.concatenate([ang, ang], axis=-1)
    cos, sin = jnp.cos(ang), jnp.sin(ang)
    half = ATT_DH // 2
    sin_signed = jnp.concatenate([-sin[:, :half], sin[:, half:]], axis=-1)
    return jnp.tile(cos, (1, 2)), jnp.tile(sin_signed, (1, 2))


def kernel(x, ffn1_norm, ffn1_w_gate, ffn1_w_up, ffn1_w_down, mix_norm, w_in, conv_w, conv_b, gate_a_w, gate_a_b, gate_x_w, gate_x_b, lru_param, rnn_out_norm, q_norm, k_norm, lam_q1, lam_k1, lam_q2, lam_k2, subln, w_out, ffn2_norm, ffn2_w_gate, ffn2_w_up, ffn2_w_down):
    bsz, seq, d = x.shape
    depth = w_in.shape[0]
    d_rnn = conv_w.shape[-1]
    d_q = ATT_HEADS * 2 * ATT_DH
    assert w_in.shape[-1] == 2 * d_rnn + 2 * d_q + ATT_HEADS * ATT_DV
    assert d_rnn % LANES == 0 and d_q % LANES == 0
    q_col = 2 * d_rnn // ATT_DV
    k_col = q_col + d_q // ATT_DV
    v_col = k_col + d_q // ATT_DV

    cos, sin_signed = _rope_tables(seq)
    xf = x.reshape(bsz * seq, d)
    rows = lambda v: v.reshape(depth, 1, -1)
    ffn1_norm, ffn2_norm, mix_norm = rows(ffn1_norm), rows(ffn2_norm), rows(mix_norm)
    conv_b, lru_param, rnn_out_norm = rows(conv_b), rows(lru_param), rows(rnn_out_norm)
    gate_a_b, gate_x_b, subln = rows(gate_a_b), rows(gate_x_b), rows(subln)
    q_gain = rows(jnp.tile(q_norm, (1, 2)))
    k_gain = rows(jnp.tile(k_norm, (1, 2)))
    lam_params = jnp.stack([lam_q1, lam_k1, lam_q2, lam_k2], axis=1)

    for l in range(depth):
        xf = _ffn(xf, ffn1_norm, ffn1_w_gate, ffn1_w_up, ffn1_w_down,
                  layer=l, tm=1024, tf=512)

        p = _mix_in(xf, mix_norm, w_in, layer=l, tm=1024, tn=1024, row_chunks=4)

        lam_init = 0.8 - 0.6 * math.exp(-0.3 * l)
        out_rnn, out_att = _mixers(
            p, conv_w, conv_b,
            _block_diag_slabs(gate_a_w[l]).astype(BF16), gate_a_b,
            _block_diag_slabs(gate_x_w[l]).astype(BF16), gate_x_b,
            lru_param, rnn_out_norm, cos, sin_signed, q_gain, k_gain, lam_params, subln,
            layer=l, batch=bsz, seq=seq, ts=512, tq=256, lam_init=lam_init,
            q_col=q_col, k_col=k_col, v_col=v_col)

        xf = _out_proj(xf, out_rnn, out_att, w_out, layer=l, tm=512)

        xf = _ffn(xf, ffn2_norm, ffn2_w_gate, ffn2_w_up, ffn2_w_down,
                  layer=l, tm=1024, tf=512)
    return xf.reshape(bsz, seq, d)
```

```python
import functools
import math

import jax
import jax.numpy as jnp
from jax import lax
from jax.experimental import pallas as pl
from jax.experimental.pallas import tpu as pltpu

CONV_W = 4
LRU_C = 8.0
ATT_HEADS = 8
ATT_DH = 64
ATT_DV = 2 * ATT_DH
ROPE_THETA = 10000.0
EPS = 1e-6
NEG_INF = -1e30

LANES = 128
SUBLANES = 8
BF16_SUBLANES = 16
MXU_DIM = 256
VMEM_CAPACITY_BYTES = 64 * 1024 * 1024
VMEM_LIMIT_BYTES = 58 * 1024 * 1024
VMEM_SPILL_HEADROOM_BYTES = 8 * 1024 * 1024

F32 = jnp.float32
BF16 = jnp.bfloat16


def _rms_scale(x):
    return lax.rsqrt(jnp.mean(x * x, axis=-1, keepdims=True) + EPS)


def _layer_row_spec(layer, width):
    return pl.BlockSpec((None, 1, width), lambda i, j: (layer, 0, 0))


def _on_staged_row_tile(x_hbm, xs_ref, sem, first_step, later_step):
    i, j = pl.program_id(0), pl.program_id(1)
    tm = xs_ref.shape[0]

    def x_tile_copy(tile):
        start = pl.multiple_of(tile * tm, tm)
        return pltpu.make_async_copy(x_hbm.at[pl.ds(start, tm), :], xs_ref, sem)

    @pl.when((i == 0) & (j == 0))
    def _():
        x_tile_copy(0).start()

    @pl.when(j == 0)
    def _():
        x_tile_copy(i).wait()
        first_step(xs_ref[...])

    @pl.when((j == 1) & (i + 1 < pl.num_programs(0)))
    def _():
        x_tile_copy(i + 1).start()

    @pl.when(j > 0)
    def _():
        later_step()


WG_BUFS = 3


def _ffn_kernel(x_hbm, gain_ref, wg_hbm, wu_ref, wd_ref, o_ref, xs_ref, h_ref, wg_buf, sem,
                wg_sem, *, layer):
    nj = pl.num_programs(1)
    total = pl.num_programs(0) * nj
    t = pl.program_id(0) * nj + pl.program_id(1)
    tf = wg_buf.shape[2]

    def wg_copy(step):
        col = pl.multiple_of(lax.rem(step, nj) * tf, tf)
        slot = lax.rem(step, WG_BUFS)
        return pltpu.make_async_copy(wg_hbm.at[layer, :, pl.ds(col, tf)], wg_buf.at[slot],
                                     wg_sem.at[slot])

    @pl.when(t == 0)
    def _():
        for step in range(WG_BUFS - 1):
            wg_copy(step).start()

    @pl.when(t + (WG_BUFS - 1) < total)
    def _():
        wg_copy(t + (WG_BUFS - 1)).start()

    wg_copy(t).wait()

    def hidden_chunk():
        h = h_ref[...]
        g = jnp.dot(h, wg_buf[lax.rem(t, WG_BUFS)].astype(BF16), preferred_element_type=F32)
        u = jnp.dot(h, wu_ref[...].astype(BF16), preferred_element_type=F32)
        a = (g * jax.nn.sigmoid(g)) * u * 0.5
        return jnp.dot(a.astype(BF16), wd_ref[...].astype(BF16), preferred_element_type=F32)

    def first_step(x):
        h_ref[...] = (x * _rms_scale(x) * gain_ref[...]).astype(BF16)
        o_ref[...] = x + hidden_chunk()

    def later_step():
        o_ref[...] += hidden_chunk()

    _on_staged_row_tile(x_hbm, xs_ref, sem, first_step, later_step)


def _ffn(x, gain, wg, wu, wd, *, layer, tm, tf):
    m, d = x.shape
    f = wg.shape[2]
    assert f // tf >= 2
    window_bytes = 2 * tm * d * 4 + 7 * d * tf * 4 + tm * d * 4 + tm * d * 2
    vmem_limit = window_bytes + VMEM_SPILL_HEADROOM_BYTES
    assert vmem_limit <= VMEM_CAPACITY_BYTES
    return pl.pallas_call(
        functools.partial(_ffn_kernel, layer=layer),
        grid=(m // tm, f // tf),
        in_specs=[
            pl.BlockSpec(memory_space=pl.ANY),
            _layer_row_spec(layer, d),
            pl.BlockSpec(memory_space=pl.ANY),
            pl.BlockSpec((None, d, tf), lambda i, j: (layer, 0, j)),
            pl.BlockSpec((None, tf, d), lambda i, j: (layer, j, 0)),
        ],
        out_specs=pl.BlockSpec((tm, d), lambda i, j: (i, 0)),
        out_shape=jax.ShapeDtypeStruct((m, d), F32),
        scratch_shapes=[pltpu.VMEM((tm, d), F32), pltpu.VMEM((tm, d), BF16),
                        pltpu.VMEM((WG_BUFS, d, tf), F32), pltpu.SemaphoreType.DMA(()),
                        pltpu.SemaphoreType.DMA((WG_BUFS,))],
        compiler_params=pltpu.CompilerParams(
            dimension_semantics=("arbitrary", "arbitrary"),
            vmem_limit_bytes=vmem_limit,
        ),
        name="ffn",
    )(x, gain, wg, wu, wd)


def _mix_in_kernel(x_hbm, gain_ref, w_ref, o_ref, xs_ref, h_ref, wb_ref, sem, *, row_chunks):
    i, j = pl.program_id(0), pl.program_id(1)

    @pl.when(i == 0)
    def _():
        wb_ref[j] = w_ref[...].astype(BF16)

    rc = h_ref.shape[0] // row_chunks

    def column_step(x):
        w = wb_ref[j]
        for r in range(row_chunks):
            rows = slice(r * rc, (r + 1) * rc)
            if x is not None:
                xr = x[rows, :]
                h_ref[rows, :] = (xr * _rms_scale(xr) * gain_ref[...]).astype(BF16)
            o_ref[rows, :] = jnp.dot(h_ref[rows, :], w, preferred_element_type=F32)

    _on_staged_row_tile(x_hbm, xs_ref, sem, column_step, functools.partial(column_step, None))


def _mix_in(x, gain, w, *, layer, tm, tn, row_chunks):
    m, d = x.shape
    n = w.shape[2]
    nj = n // tn
    assert nj >= 2
    window_bytes = tm * d * 4 + tm * d * 2 + d * n * 2 + 2 * d * tn * 4 + 2 * tm * tn * 4
    vmem_limit = window_bytes + VMEM_SPILL_HEADROOM_BYTES
    assert vmem_limit <= VMEM_CAPACITY_BYTES
    return pl.pallas_call(
        functools.partial(_mix_in_kernel, row_chunks=row_chunks),
        grid=(m // tm, nj),
        in_specs=[
            pl.BlockSpec(memory_space=pl.ANY),
            _layer_row_spec(layer, d),
            pl.BlockSpec((None, d, tn),
                         lambda i, j: (layer, 0, jnp.where(i == 0, j, nj - 1))),
        ],
        out_specs=pl.BlockSpec((tm, tn), lambda i, j: (i, j)),
        out_shape=jax.ShapeDtypeStruct((m, n), F32),
        scratch_shapes=[pltpu.VMEM((tm, d), F32), pltpu.VMEM((tm, d), BF16),
                        pltpu.VMEM((nj, d, tn), BF16), pltpu.SemaphoreType.DMA(())],
        compiler_params=pltpu.CompilerParams(
            dimension_semantics=("arbitrary", "arbitrary"),
            vmem_limit_bytes=vmem_limit,
        ),
        name="mix_in",
    )(x, gain, w)


def _rnn_kernel(x_ref, g_ref, cw_ref, cb_ref, wa_ref, ba_ref, wx_ref, bx_ref,
                lru_ref, gain_ref, o_ref, xs_ref, hprev_ref):
    ts, c = x_ref.shape

    @pl.when(pl.program_id(1) == 0)
    def _():
        xs_ref[0:SUBLANES, :] = jnp.zeros((SUBLANES, c), F32)
        hprev_ref[...] = jnp.zeros_like(hprev_ref)

    x = x_ref[...]
    xs_ref[SUBLANES:, :] = x
    xc = cb_ref[...] + x * cw_ref[CONV_W - 1:CONV_W, :]
    for k in range(CONV_W - 1):
        start = SUBLANES - (CONV_W - 1 - k)
        xc = xc + xs_ref[start:start + ts, :] * cw_ref[k:k + 1, :]
    xs_ref[0:SUBLANES, :] = x[ts - SUBLANES:, :]

    xcb = xc.astype(BF16)
    n_slabs = c // MXU_DIM

    def gate(w_ref, b_ref):
        parts = [
            jnp.dot(xcb[:, s * MXU_DIM:(s + 1) * MXU_DIM], w_ref[s],
                    preferred_element_type=F32)
            for s in range(n_slabs)
        ]
        return jax.nn.sigmoid(jnp.concatenate(parts, axis=1) + b_ref[...])

    r = gate(wa_ref, ba_ref)
    i = gate(wx_ref, bx_ref)

    neg_l = -lru_ref[...]
    softplus = jnp.maximum(neg_l, 0.0) + jnp.log1p(jnp.exp(-jnp.abs(neg_l)))
    log_a = (-LRU_C * r) * softplus
    a = jnp.exp(log_a)
    mult = jnp.sqrt(-jnp.tanh(log_a) * (1.0 + a * a))
    u = mult * (i * xc)

    sub = lax.broadcasted_iota(jnp.int32, (SUBLANES, c), 0)
    carry = hprev_ref[0:1, :]
    groups = []
    for g in range(ts // SUBLANES):
        rows = slice(g * SUBLANES, (g + 1) * SUBLANES)
        acc_a, acc_h = a[rows, :], u[rows, :]
        k = 1
        while k < SUBLANES:
            keep = sub >= k
            sh_a = pltpu.roll(acc_a, k, axis=0)
            sh_h = pltpu.roll(acc_h, k, axis=0)
            acc_h = jnp.where(keep, acc_a * sh_h + acc_h, acc_h)
            acc_a = jnp.where(keep, acc_a * sh_a, acc_a)
            k *= 2
        h_g = acc_h + acc_a * carry
        carry = h_g[SUBLANES - 1:SUBLANES, :]
        groups.append(h_g)
    h = jnp.concatenate(groups, axis=0)
    hprev_ref[0:1, :] = carry

    y = jax.nn.gelu(g_ref[...]) * h
    o_ref[...] = (y * _rms_scale(y) * gain_ref[...]).astype(o_ref.dtype)


def _qk_prep(t, gain, cos, sin_signed):
    lane = lax.broadcasted_iota(jnp.int32, t.shape, 1)
    lo = lane < ATT_DH
    t2 = t * t
    ss_lo = jnp.sum(jnp.where(lo, t2, 0.0), axis=-1, keepdims=True)
    ss_hi = jnp.sum(jnp.where(lo, 0.0, t2), axis=-1, keepdims=True)
    ms = jnp.where(lo, ss_lo, ss_hi) * (1.0 / ATT_DH)
    y = t * lax.rsqrt(ms + EPS) * gain
    half = ATT_DH // 2
    ahead = pltpu.roll(y, LANES - half, axis=1)
    behind = pltpu.roll(y, half, axis=1)
    rot = jnp.where((lane & half) == 0, ahead, behind)
    return y * cos + rot * sin_signed


def _attn_head(q_ref, k_ref, v_ref, cos_ref, sin_ref, qn_ref, kn_ref, lam_ref,
               subln_ref, o_ref, kh_ref, vt_ref, *, tq, lam_init):
    seq = q_ref.shape[0]
    kh_ref[...] = _qk_prep(k_ref[...].astype(F32), kn_ref[...], cos_ref[...],
                           sin_ref[...]).astype(BF16)
    vt_ref[0:ATT_DV, :] = v_ref[...].astype(F32).T.astype(BF16)
    vt_ref[ATT_DV:, :] = jnp.ones((vt_ref.shape[0] - ATT_DV, seq), BF16)

    lp = lam_ref[...]
    lam = (jnp.exp(jnp.sum(lp[0:1] * lp[1:2], axis=-1, keepdims=True))
           - jnp.exp(jnp.sum(lp[2:3] * lp[3:4], axis=-1, keepdims=True))
           + lam_init)

    lane = lax.broadcasted_iota(jnp.int32, (tq, ATT_DV), 1)
    lo = lane < ATT_DH
    causal_t = (lax.broadcasted_iota(jnp.int32, (tq, 2 * tq), 0)
                <= (lax.broadcasted_iota(jnp.int32, (tq, 2 * tq), 1) & (tq - 1)))
    nt_dims = (((1,), (1,)), ((), ()))

    def score_phase(blk, out):
        rows = slice(blk * tq, (blk + 1) * tq)
        qh = _qk_prep(q_ref[rows, :].astype(F32), qn_ref[...], cos_ref[rows, :],
                      sin_ref[rows, :])
        qh = qh * (math.log2(math.e) / math.sqrt(ATT_DH))
        q_both = jnp.concatenate([jnp.where(lo, qh, 0.0), jnp.where(lo, 0.0, qh)],
                                 axis=0).astype(BF16)
        chunks, m = [], None
        for c in range(blk + 1):
            st = lax.dot_general(kh_ref[c * tq:(c + 1) * tq, :], q_both, nt_dims,
                                 preferred_element_type=F32)
            if c == blk:
                st = jnp.where(causal_t, st, NEG_INF)
            cm = jnp.max(st, axis=0, keepdims=True)
            m = cm if m is None else jnp.maximum(m, cm)
            chunks.append(st)
            yield
        out.append((chunks, m))

    def value_phase(blk, chunks, m):
        acc = None
        for c, st in enumerate(chunks):
            part = jnp.dot(vt_ref[:, c * tq:(c + 1) * tq], jnp.exp2(st - m).astype(BF16),
                           preferred_element_type=F32)
            acc = part if acc is None else acc + part
            yield
        l = acc[ATT_DV:ATT_DV + 1, :]
        a1, a2 = acc[0:ATT_DV, 0:tq], acc[0:ATT_DV, tq:]
        l1, l2 = l[:, 0:tq], l[:, tq:]
        ot = a1 * (1.0 / l1) - a2 * (lam / l2)
        ot = ot * lax.rsqrt(jnp.mean(ot * ot, axis=0, keepdims=True) + EPS)
        o = ot.T * (subln_ref[...] * (1.0 - lam_init))
        o_ref[blk * tq:(blk + 1) * tq, :] = o.astype(o_ref.dtype)

    n_blk = seq // tq
    scored = []
    for _ in score_phase(0, scored):
        pass
    for blk in range(n_blk):
        phases = [value_phase(blk, *scored[blk])]
        if blk + 1 < n_blk:
            phases.append(score_phase(blk + 1, scored))
        while phases:
            for ph in list(phases):
                if next(ph, StopIteration) is StopIteration:
                    phases.remove(ph)


def _attn_kernel(q_ref, k_ref, v_ref, cos_ref, sin_ref, qn_ref, kn_ref, lam_ref,
                 subln_ref, o_ref, kh_ref, vt_ref, *, tq, lam_init):
    for hh in range(kh_ref.shape[0]):
        cols = slice(hh * ATT_DV, (hh + 1) * ATT_DV)
        _attn_head(q_ref.at[:, cols], k_ref.at[:, cols], v_ref.at[:, cols], cos_ref, sin_ref,
                   qn_ref, kn_ref, lam_ref, subln_ref, o_ref.at[:, cols], kh_ref.at[hh],
                   vt_ref.at[hh], tq=tq, lam_init=lam_init)


N_RNN_IN = 10


def _mixers_kernel(*refs, tq, lam_init):
    rnn_in, att_in = refs[:N_RNN_IN], refs[N_RNN_IN:-6]
    rnn_out, att_out, xs_ref, hprev_ref, kh_ref, vt_ref = refs[-6:]
    _rnn_kernel(*rnn_in, rnn_out, xs_ref, hprev_ref)
    _attn_kernel(*att_in, att_out, kh_ref, vt_ref, tq=tq, lam_init=lam_init)


def _mixers(p, conv_w, conv_b, wa_bd, ba, wx_bd, bx, lru, rnn_gain, cos, sin_signed, qn, kn,
            lam_params, subln, *, layer, batch, seq, ts, tq, lam_init, q_col, k_col, v_col):
    c = conv_w.shape[2]
    d = ATT_DV
    nt = seq // ts
    assert ATT_HEADS % nt == 0
    hp = ATT_HEADS // nt
    assert q_col % hp == 0 and k_col % hp == 0 and v_col % hp == 0
    row = lambda b, s: (b * nt + s, 0)
    gate_row = lambda b, s: (b * nt + s, 1)
    const2 = lambda b, s: (0, 0)
    const3 = lambda b, s: (0, 0, 0)
    vec_c = _layer_row_spec(layer, c)
    vec_d = _layer_row_spec(layer, d)
    rnn_specs = [
        pl.BlockSpec((ts, c), row),
        pl.BlockSpec((ts, c), gate_row),
        pl.BlockSpec((None, CONV_W, c), lambda b, s: (layer, 0, 0)),
        vec_c,
        pl.BlockSpec(wa_bd.shape, const3),
        vec_c,
        pl.BlockSpec(wx_bd.shape, const3),
        vec_c,
        vec_c,
        vec_c,
    ]
    assert len(rnn_specs) == N_RNN_IN
    att_specs = [
        pl.BlockSpec((seq, hp * d), lambda b, s: (b, q_col // hp + s)),
        pl.BlockSpec((seq, hp * d), lambda b, s: (b, k_col // hp + s)),
        pl.BlockSpec((seq, hp * d), lambda b, s: (b, v_col // hp + s)),
        pl.BlockSpec((seq, d), const2),
        pl.BlockSpec((seq, d), const2),
        vec_d,
        vec_d,
        pl.BlockSpec((None, 4, ATT_DH), lambda b, s: (layer, 0, 0)),
        vec_d,
    ]
    return pl.pallas_call(
        functools.partial(_mixers_kernel, tq=tq, lam_init=lam_init),
        grid=(batch, nt),
        in_specs=rnn_specs + att_specs,
        out_specs=[pl.BlockSpec((ts, c), row),
                   pl.BlockSpec((seq, hp * d), lambda b, s: (b, s))],
        out_shape=[jax.ShapeDtypeStruct((batch * seq, c), BF16),
                   jax.ShapeDtypeStruct((batch * seq, ATT_HEADS * d), BF16)],
        scratch_shapes=[pltpu.VMEM((SUBLANES + ts, c), F32), pltpu.VMEM((SUBLANES, c), F32),
                        pltpu.VMEM((hp, seq, d), BF16),
                        pltpu.VMEM((hp, d + BF16_SUBLANES, seq), BF16)],
        compiler_params=pltpu.CompilerParams(
            dimension_semantics=("arbitrary", "arbitrary"),
            vmem_limit_bytes=VMEM_LIMIT_BYTES,
        ),
        name="mixers",
    )(p, p, conv_w, conv_b, wa_bd, ba, wx_bd, bx, lru, rnn_gain,
      p, p, p, cos, sin_signed, qn, kn, lam_params, subln)


def _out_proj_kernel(x_ref, rnn_ref, att_ref, w_ref, o_ref, wb_ref):
    @pl.when(pl.program_id(0) == 0)
    def _():
        wb_ref[...] = w_ref[...].astype(BF16)

    c = rnn_ref.shape[1]
    mixed = jnp.dot(rnn_ref[...], wb_ref[:c, :], preferred_element_type=F32)
    mixed = mixed + jnp.dot(att_ref[...], wb_ref[c:, :], preferred_element_type=F32)
    o_ref[...] = x_ref[...] + mixed


def _out_proj(x, rnn, att, w, *, layer, tm):
    m, d = x.shape
    c = rnn.shape[1]
    k = w.shape[1]
    return pl.pallas_call(
        _out_proj_kernel,
        grid=(m // tm,),
        in_specs=[
            pl.BlockSpec((tm, d), lambda i: (i, 0)),
            pl.BlockSpec((tm, c), lambda i: (i, 0)),
            pl.BlockSpec((tm, att.shape[1]), lambda i: (i, 0)),
            pl.BlockSpec((None, k, d), lambda i: (layer, 0, 0),
                         pipeline_mode=pl.Buffered(1)),
        ],
        out_specs=pl.BlockSpec((tm, d), lambda i: (i, 0)),
        out_shape=jax.ShapeDtypeStruct((m, d), F32),
        scratch_shapes=[pltpu.VMEM((k, d), BF16)],
        compiler_params=pltpu.CompilerParams(
            dimension_semantics=("arbitrary",),
            vmem_limit_bytes=VMEM_LIMIT_BYTES,
        ),
        name="out_proj",
    )(x, rnn, att, w)


def _block_diag_slabs(w):
    g, n, _ = w.shape
    per = MXU_DIM // n
    w5 = w.reshape(g // per, per, n, 1, n)
    eye = jnp.eye(per, dtype=bool)[None, :, None, :, None]
    return jnp.where(eye, w5, 0.0).reshape(g // per, MXU_DIM, MXU_DIM)


def _rope_tables(seq):
    inv_freq = ROPE_THETA ** (-jnp.arange(0, ATT_DH, 2, dtype=F32) / ATT_DH)
    ang = jnp.arange(seq, dtype=F32)[:, None] * inv_freq[None, :]
    ang = jnp.concatenate([ang, ang], axis=-1)
    cos, sin = jnp.cos(ang), jnp.sin(ang)
    half = ATT_DH // 2
    sin_signed = jnp.concatenate([-sin[:, :half], sin[:, half:]], axis=-1)
    return jnp.tile(cos, (1, 2)), jnp.tile(sin_signed, (1, 2))


def kernel(x, ffn1_norm, ffn1_w_gate, ffn1_w_up, ffn1_w_down, mix_norm, w_in, conv_w, conv_b, gate_a_w, gate_a_b, gate_x_w, gate_x_b, lru_param, rnn_out_norm, q_norm, k_norm, lam_q1, lam_k1, lam_q2, lam_k2, subln, w_out, ffn2_norm, ffn2_w_gate, ffn2_w_up, ffn2_w_down):
    bsz, seq, d = x.shape
    depth = w_in.shape[0]
    d_rnn = conv_w.shape[-1]
    d_q = ATT_HEADS * 2 * ATT_DH
    assert w_in.shape[-1] == 2 * d_rnn + 2 * d_q + ATT_HEADS * ATT_DV
    assert d_rnn % LANES == 0 and d_q % LANES == 0
    q_col = 2 * d_rnn // ATT_DV
    k_col = q_col + d_q // ATT_DV
    v_col = k_col + d_q // ATT_DV

    cos, sin_signed = _rope_tables(seq)
    xf = x.reshape(bsz * seq, d)
    rows = lambda v: v.reshape(depth, 1, -1)
    ffn1_norm, ffn2_norm, mix_norm = rows(ffn1_norm), rows(ffn2_norm), rows(mix_norm)
    conv_b, lru_param, rnn_out_norm = rows(conv_b), rows(lru_param), rows(rnn_out_norm)
    gate_a_b, gate_x_b, subln = rows(gate_a_b), rows(gate_x_b), rows(subln)
    q_gain = rows(jnp.tile(q_norm, (1, 2)))
    k_gain = rows(jnp.tile(k_norm, (1, 2)))
    lam_params = jnp.stack([lam_q1, lam_k1, lam_q2, lam_k2], axis=1)

    for l in range(depth):
        xf = _ffn(xf, ffn1_norm, ffn1_w_gate, ffn1_w_up, ffn1_w_down,
                  layer=l, tm=1024, tf=512)

        p = _mix_in(xf, mix_norm, w_in, layer=l, tm=1024, tn=1024, row_chunks=4)

        lam_init = 0.8 - 0.6 * math.exp(-0.3 * l)
        out_rnn, out_att = _mixers(
            p, conv_w, conv_b,
            _block_diag_slabs(gate_a_w[l]).astype(BF16), gate_a_b,
            _block_diag_slabs(gate_x_w[l]).astype(BF16), gate_x_b,
            lru_param, rnn_out_norm, cos, sin_signed, q_gain, k_gain, lam_params, subln,
            layer=l, batch=bsz, seq=seq, ts=512, tq=256, lam_init=lam_init,
            q_col=q_col, k_col=k_col, v_col=v_col)

        xf = _out_proj(xf, out_rnn, out_att, w_out, layer=l, tm=512)

        xf = _ffn(xf, ffn2_norm, ffn2_w_gate, ffn2_w_up, ffn2_w_down,
                  layer=l, tm=1024, tf=512)
    return xf.reshape(bsz, seq, d)
```
